```python
import jax, jax.numpy as jnp
from jax import lax
import numpy as np

D_MODEL = 1024
BATCH = 2
SEQ = 8192
DEPTH = 4

MEM_LEN = 256
A_WIDTH = 512
A_GROUPS = 4
A_CHUNK = 128
B_WIDTH = 512
B_KERNEL = 31
C_HEAD = 64
C_HEADS = 8
C_WIDTH = C_HEADS * C_HEAD
C_DECAY_RANK = 64
C_AAA_RANK = 64
C_VRES_RANK = 32
C_SHIFT_WIDTH = 3 * C_WIDTH + C_DECAY_RANK + C_AAA_RANK
X_HEADS = 4
X_HEAD = 64
X_WIDTH = X_HEADS * X_HEAD
N_BRANCH = 4
IN_WIDTHS = (2 * A_WIDTH, A_WIDTH, 2 * B_WIDTH, B_WIDTH, C_SHIFT_WIDTH, C_WIDTH, X_WIDTH, N_BRANCH * D_MODEL)
N_MAIN = 2 * A_WIDTH + A_WIDTH + 2 * B_WIDTH + B_WIDTH + C_SHIFT_WIDTH + C_WIDTH + X_WIDTH + N_BRANCH * D_MODEL

RMS_EPS = 1e-6
LN_EPS = 1e-5
GN_EPS = 64e-5
L2_EPS = 1e-12

kernel_name = "hybrid_sgu_conformer_rwkv7_memxattn"


def rmsnorm(x, g):
    xf = x.astype(jnp.float32)
    y = xf * lax.rsqrt(jnp.mean(xf * xf, axis=-1, keepdims=True) + RMS_EPS)
    return (y * g.astype(jnp.float32)).astype(x.dtype)


def layernorm(x, g, b, eps=LN_EPS):
    xf = x.astype(jnp.float32)
    mu = jnp.mean(xf, axis=-1, keepdims=True)
    var = jnp.mean(jnp.square(xf - mu), axis=-1, keepdims=True)
    y = (xf - mu) * lax.rsqrt(var + eps)
    return (y * g.astype(jnp.float32) + b.astype(jnp.float32)).astype(x.dtype)


def shift_right(z):
    return jnp.pad(z, ((0, 0), (1, 0), (0, 0)))[:, :-1]


def split_columns(z):
    cuts, acc = [], 0
    for w in IN_WIDTHS[:-1]:
        acc += w
        cuts.append(acc)
    return jnp.split(z, cuts, axis=-1)


def spatial_gating(z_uv, ln_g, ln_b, w_s, b_s):
    bsz, seq, _ = z_uv.shape
    z = jax.nn.gelu(z_uv)
    u, v = jnp.split(z, 2, axis=-1)
    v = layernorm(v, ln_g, ln_b)
    v = v.reshape(bsz, seq // A_CHUNK, A_CHUNK, A_GROUPS, A_WIDTH // A_GROUPS)
    causal = jnp.tril(jnp.ones((A_CHUNK, A_CHUNK), dtype=bool))
    w = jnp.where(causal[None], w_s, 0)
    s = jnp.einsum('gts,bnsgc->bntgc', w, v) + b_s.T[:, :, None]
    return u * s.reshape(bsz, seq, A_WIDTH)


def conformer_conv(z_glu, w_dw, b_dw, ln_g, ln_b):
    a, g = jnp.split(z_glu, 2, axis=-1)
    h = a * jax.nn.sigmoid(g)
    h = lax.conv_general_dilated(
        h, w_dw[:, None, :], window_strides=(1,), padding=((B_KERNEL - 1, 0),),
        dimension_numbers=('NWC', 'WIO', 'NWC'), feature_group_count=B_WIDTH) + b_dw
    h = layernorm(h, ln_g, ln_b)
    return jax.nn.silu(h)


def wkv7_scan(r, decay, k, v, kk, a):
    def step(state, inp):
        r_t, w_t, k_t, v_t, kk_t, a_t = inp
        sa = jnp.einsum('bhvk,bhk->bhv', state, -kk_t)
        state = (state * w_t[:, :, None, :]
                 + sa[..., None] * (kk_t * a_t)[:, :, None, :]
                 + v_t[..., None] * k_t[:, :, None, :])
        return state, jnp.einsum('bhvk,bhk->bhv', state, r_t)
    bsz, seq, nh, hd = r.shape
    s0 = jnp.zeros((bsz, nh, hd, hd), jnp.float32)
    xs = tuple(jnp.swapaxes(t, 0, 1) for t in (r, decay, k, v, kk, a))
    _, o = lax.scan(step, s0, xs)
    return jnp.swapaxes(o, 0, 1)


def rwkv7_time_mix(zc, v_first, v_mix, w0, w_w2, a0, w_a2, k_k, k_a, r_k, gn_g, gn_b):
    bsz, seq, _ = zc.shape
    r, k, v, w_lo, a_lo = jnp.split(
        zc, [C_WIDTH, 2 * C_WIDTH, 3 * C_WIDTH, 3 * C_WIDTH + C_DECAY_RANK], axis=-1)
    w_log = -jax.nn.softplus(-(w0 + jnp.tanh(w_lo) @ w_w2)) - 0.5
    decay = jnp.exp(-jnp.exp(w_log.astype(jnp.float32)))
    a = jax.nn.sigmoid(a0 + a_lo @ w_a2)
    if v_mix is None:
        v_first = v
    else:
        v = v + (v_first - v) * v_mix
    heads = lambda t: t.reshape(bsz, seq, C_HEADS, C_HEAD).astype(jnp.float32)
    r, k, v, a, decay = heads(r), heads(k), heads(v), heads(a), heads(decay)
    kk = k * k_k.reshape(C_HEADS, C_HEAD)
    kk = kk * lax.rsqrt(jnp.sum(kk * kk, axis=-1, keepdims=True) + L2_EPS)
    k = k * (1.0 + (a - 1.0) * k_a.reshape(C_HEADS, C_HEAD))
    o = wkv7_scan(r, decay, k, v, kk, a)
    o = layernorm(o, gn_g, gn_b, GN_EPS)
    o = o + jnp.sum(r * k * r_k, axis=-1, keepdims=True) * v
    return o.reshape(bsz, seq, C_WIDTH).astype(zc.dtype), v_first


def memory_attention(q, mem_kv):
    bsz, seq, _ = q.shape
    q = q.reshape(bsz, seq, X_HEADS, X_HEAD)
    km, vm = jnp.split(mem_kv, 2, axis=-1)
    km = km.reshape(bsz, -1, X_HEADS, X_HEAD)
    vm = vm.reshape(bsz, -1, X_HEADS, X_HEAD)
    s = jnp.einsum('bshd,bmhd->bhsm', q, km).astype(jnp.float32) * (X_HEAD ** -0.5)
    p = jax.nn.softmax(s, axis=-1).astype(q.dtype)
    o = jnp.einsum('bhsm,bmhd->bshd', p, vm)
    return o.reshape(bsz, seq, X_WIDTH)


def setup_inputs(seed: int = 0) -> dict:
    key = jax.random.key(seed)
    ks = iter(jax.random.split(key, 40))
    nrm = lambda shape, scale: scale * jax.random.normal(next(ks), shape, jnp.float32)
    uni = lambda shape, lo, hi: jax.random.uniform(next(ks), shape, jnp.float32, lo, hi)
    L, Lv, D = DEPTH, DEPTH - 1, D_MODEL
    return {
        "x": nrm((BATCH, SEQ, D), 1.0),
        "mem": nrm((BATCH, MEM_LEN, D), 1.0),
        "norm_g": 1.0 + nrm((L, D), 0.02),
        "w_in": nrm((L, D, N_MAIN), D ** -0.5),
        "w_in_vres": nrm((Lv, D, C_VRES_RANK), D ** -0.5),
        "a_ln_g": 1.0 + nrm((L, A_WIDTH), 0.02),
        "a_ln_b": nrm((L, A_WIDTH), 0.02),
        "a_ws": nrm((L, A_GROUPS, A_CHUNK, A_CHUNK), A_CHUNK ** -0.5),
        "a_bs": 1.0 + nrm((L, A_GROUPS, A_CHUNK), 0.1),
        "b_wdw": nrm((L, B_KERNEL, B_WIDTH), B_KERNEL ** -0.5),
        "b_bdw": nrm((L, B_WIDTH), 0.02),
        "b_ln_g": 1.0 + nrm((L, B_WIDTH), 0.02),
        "b_ln_b": nrm((L, B_WIDTH), 0.02),
        "c_mu": uni((L, C_SHIFT_WIDTH), 0.0, 1.0),
        "c_mu_vres": uni((Lv, C_VRES_RANK), 0.0, 1.0),
        "c_w0": uni((L, C_WIDTH), -4.0, 1.0),
        "c_ww2": nrm((L, C_DECAY_RANK, C_WIDTH), 0.5 * C_DECAY_RANK ** -0.5),
        "c_a0": nrm((L, C_WIDTH), 0.1),
        "c_wa2": nrm((L, C_AAA_RANK, C_WIDTH), 0.5 * C_AAA_RANK ** -0.5),
        "c_v0": nrm((Lv, C_WIDTH), 0.1),
        "c_wv2": nrm((Lv, C_VRES_RANK, C_WIDTH), 0.5 * C_VRES_RANK ** -0.5),
        "c_kk": 0.85 + nrm((L, C_WIDTH), 0.05),
        "c_ka": 1.0 + nrm((L, C_WIDTH), 0.05),
        "c_rk": nrm((L, C_HEADS, C_HEAD), 0.1),
        "c_gn_g": 1.0 + nrm((L, C_HEADS, C_HEAD), 0.02),
        "c_gn_b": nrm((L, C_HEADS, C_HEAD), 0.02),
        "mem_norm_g": 1.0 + nrm((D,), 0.02),
        "w_mem_kv": nrm((L, D, 2 * X_WIDTH), D ** -0.5),
        "w_branch_a": nrm((L, A_WIDTH, D), A_WIDTH ** -0.5),
        "w_branch_b": nrm((L, B_WIDTH, D), B_WIDTH ** -0.5),
        "w_branch_c": nrm((L, C_WIDTH, D), C_WIDTH ** -0.5),
        "w_branch_x": nrm((L, X_WIDTH, D), X_WIDTH ** -0.5),
        "w_out": nrm((L, D, D), 0.5 * D ** -0.5),
        "final_norm_g": 1.0 + nrm((D,), 0.02),
    }


def reference(x, mem, norm_g, w_in, w_in_vres, a_ln_g, a_ln_b, a_ws, a_bs,
              b_wdw, b_bdw, b_ln_g, b_ln_b, c_mu, c_mu_vres, c_w0, c_ww2, c_a0, c_wa2,
              c_v0, c_wv2, c_kk, c_ka, c_rk, c_gn_g, c_gn_b, mem_norm_g, w_mem_kv,
              w_branch_a, w_branch_b, w_branch_c, w_branch_x, w_out, final_norm_g):
    bsz, seq, _ = x.shape
    mem_n = rmsnorm(mem, mem_norm_g)
    v_first = None
    for i in range(DEPTH):
        h = rmsnorm(x, norm_g[i])
        w_cat = w_in[i] if i == 0 else jnp.concatenate([w_in[i], w_in_vres[i - 1]], axis=-1)
        z = h @ w_cat
        za_uv, za_gate, zb_glu, zb_gate, zc, zc_gate, zx_q, z_merge = split_columns(z[..., :N_MAIN])

        ya = spatial_gating(za_uv, a_ln_g[i], a_ln_b[i], a_ws[i], a_bs[i]) * jax.nn.silu(za_gate)

        yb = conformer_conv(zb_glu, b_wdw[i], b_bdw[i], b_ln_g[i], b_ln_b[i]) * jax.nn.silu(zb_gate)

        zc = zc + c_mu[i] * (shift_right(zc) - zc)
        if i == 0:
            v_mix = None
        else:
            zv = z[..., N_MAIN:]
            zv = zv + c_mu_vres[i - 1] * (shift_right(zv) - zv)
            v_mix = jax.nn.sigmoid(c_v0[i - 1] + zv @ c_wv2[i - 1])
        yc, v_first = rwkv7_time_mix(zc, v_first, v_mix, c_w0[i], c_ww2[i], c_a0[i], c_wa2[i],
                                     c_kk[i], c_ka[i], c_rk[i], c_gn_g[i], c_gn_b[i])
        yc = yc * jax.nn.silu(zc_gate)

        yx = memory_attention(zx_q, mem_n @ w_mem_kv[i])

        gates = jax.nn.sigmoid(z_merge).reshape(bsz, seq, N_BRANCH, D_MODEL)
        merged = (gates[:, :, 0] * (ya @ w_branch_a[i])
                  + gates[:, :, 1] * (yb @ w_branch_b[i])
                  + gates[:, :, 2] * (yc @ w_branch_c[i])
                  + gates[:, :, 3] * (yx @ w_branch_x[i]))
        x = x + merged @ w_out[i]
    return rmsnorm(x, final_norm_g)
```

```python
import functools

import jax
import jax.numpy as jnp
from jax import lax
from jax.experimental import pallas as pl
from jax.experimental.pallas import tpu as pltpu

F32 = jnp.float32
BF16 = jnp.bfloat16

D_MODEL = 1024
MEM_LEN = 256
A_WIDTH = 512
A_GROUPS = 4
A_CHUNK = 128
B_WIDTH = 512
B_KERNEL = 31
C_HEAD = 64
C_HEADS = 8
C_WIDTH = C_HEADS * C_HEAD
C_DECAY_RANK = 64
C_AAA_RANK = 64
C_VRES_RANK = 32
C_SHIFT_WIDTH = 3 * C_WIDTH + C_DECAY_RANK + C_AAA_RANK
X_HEADS = 4
X_HEAD = 64
X_WIDTH = X_HEADS * X_HEAD

RMS_EPS = 1e-6
LN_EPS = 1e-5
GN_EPS = 64e-5
L2_EPS = 1e-12

LANES = 128
WKV_CHUNK = 64
WKV_MID = WKV_CHUNK // 2 - 1
QUAD = 4 * C_HEAD
N_QUADS = C_WIDTH // QUAD
CONV_HIST = 32
VMEM_LIMIT = 56 * 1024 * 1024


def _split2(x):
    hi = x.astype(BF16)
    lo = (x - hi.astype(F32)).astype(BF16)
    return hi, lo


def _split3(x):
    hi = x.astype(BF16)
    r1 = x - hi.astype(F32)
    mid = r1.astype(BF16)
    lo = (r1 - mid.astype(F32)).astype(BF16)
    return hi, mid, lo


def _dot(a, b):
    return jnp.dot(a, b, preferred_element_type=F32)


def _mm(a, b):
    return _dot(a.astype(BF16), b.astype(BF16))


def _mm_nt(a, b):
    return lax.dot_general(a.astype(BF16), b.astype(BF16), (((1,), (1,)), ((), ())),
                           preferred_element_type=F32)


def _mm3(a, b):
    ah, al = _split2(a)
    bh, bl = _split2(b)
    return _dot(ah, bh) + (_dot(ah, bl) + _dot(al, bh))


def _mm_exact_rhs(a, b_bf16):
    ah, al = _split2(a)
    return _dot(ah, b_bf16) + _dot(al, b_bf16)


def _rmsnorm(x, g):
    ms = jnp.mean(x * x, axis=-1, keepdims=True)
    return x * lax.rsqrt(ms + RMS_EPS) * g


def _layernorm(x, g, b, eps):
    mu = jnp.mean(x, axis=-1, keepdims=True)
    xc = x - mu
    var = jnp.mean(xc * xc, axis=-1, keepdims=True)
    return xc * lax.rsqrt(var + eps) * g + b


def _sigmoid(x):
    return jax.nn.sigmoid(x)


def _silu(x):
    return x * jax.nn.sigmoid(x)


def _gelu_tanh(x):
    c = 0.7978845608028654
    return 0.5 * x * (1.0 + jnp.tanh(c * (x + 0.044715 * (x * x * x))))


def _head_stack(x, n_heads, head_width):
    lane_head = lax.broadcasted_iota(jnp.int32, x.shape, 1) // head_width
    zero = jnp.zeros_like(x)
    return jnp.concatenate([jnp.where(lane_head == h, x, zero) for h in range(n_heads)], axis=0)


def _shift_rows(z, prev_row):
    rolled = pltpu.roll(z, 1, 0)
    row = lax.broadcasted_iota(jnp.int32, z.shape, 0)
    return jnp.where(row == 0, prev_row, rolled)


def _memkv_kernel(mem_ref, g_ref, w_ref, k_ref, v_ref):
    mem_n = _rmsnorm(mem_ref[0], g_ref[...])
    kv = _mm(mem_n, w_ref[0])
    k_ref[0, 0] = _head_stack(kv[:, :X_WIDTH], X_HEADS, X_HEAD).astype(BF16)
    v_ref[0, 0] = _head_stack(kv[:, X_WIDTH:], X_HEADS, X_HEAD).astype(BF16)


def _memkv(mem, mem_norm_g, w_mem_kv):
    bsz = mem.shape[0]
    depth = w_mem_kv.shape[0]
    out = jax.ShapeDtypeStruct((depth, bsz, X_HEADS * MEM_LEN, X_WIDTH), BF16)
    spec = pl.BlockSpec((1, 1, X_HEADS * MEM_LEN, X_WIDTH), lambda l, b: (l, b, 0, 0))
    return pl.pallas_call(
        _memkv_kernel,
        grid=(depth, bsz),
        in_specs=[pl.BlockSpec((1, MEM_LEN, D_MODEL), lambda l, b: (b, 0, 0)),
                  pl.BlockSpec((1, D_MODEL), lambda l, b: (0, 0)),
                  pl.BlockSpec((1, D_MODEL, 2 * X_WIDTH), lambda l, b: (l, 0, 0))],
        out_specs=[spec, spec],
        out_shape=[out, out],
        name="memkv",
    )(mem, mem_norm_g.reshape(1, D_MODEL), w_mem_kv)


def _wkv_chunk(q, r, k, v, kk, b, lw, cl, state_ref):
    c = WKV_CHUNK
    cle = cl - lw
    m = cl[WKV_MID:WKV_MID + 1, :]
    cl_end = cl[c - 1:c, :]
    e_fwd = jnp.exp(cl - m)
    e_inv = jnp.exp(m - cl)
    e_mid = jnp.exp(m)
    e_end = jnp.exp(cl_end - m)
    r_t = r * e_fwd
    a_t = -kk * jnp.exp(cle - m)
    b_t = b * e_inv
    k_t = k * e_inv
    r_bar = r_t * e_mid
    a_bar = a_t * e_mid
    b_hat = b_t * e_end
    k_hat = k_t * e_end
    d_end = e_end * e_mid

    stack = functools.partial(_head_stack, n_heads=4, head_width=C_HEAD)
    row = lax.broadcasted_iota(jnp.int32, (c, QUAD), 0)
    col = lax.broadcasted_iota(jnp.int32, (c, QUAD), 1) % c
    strict = row > col
    incl = row >= col

    lhs = jnp.concatenate([a_t, r_t], axis=0)
    rhs = jnp.concatenate([stack(b_t), stack(k_t)], axis=0)
    lh, ll = _split2(lhs)
    rh, rl = _split2(rhs)
    nt = lambda x, y: lax.dot_general(x, y, (((1,), (1,)), ((), ())), preferred_element_type=F32)
    g = nt(lh, rh) + (nt(lh, rl) + nt(ll, rh))
    zero = jnp.zeros((c, QUAD), F32)
    l_ab = jnp.where(strict, g[:c, :QUAD], zero)
    l_ak = jnp.where(strict, g[:c, QUAD:], zero)
    a_rb = jnp.where(incl, g[c:, :QUAD], zero)
    a_rk = jnp.where(incl, g[c:, QUAD:], zero)

    t_inv = jnp.where(row == col, 1.0, 0.0) + l_ab
    l_pow = l_ab
    for _ in range(5):
        l_pow = _mm3(l_pow, stack(l_pow))
        t_inv = t_inv + _mm3(t_inv, stack(l_pow))

    v_stack = stack(v)
    w_v = _mm3(l_ak, v_stack)
    tva = _mm3(t_inv, jnp.concatenate([stack(w_v), stack(a_bar)], axis=1))
    tv = tva[:, :QUAD]
    a_p = tva[:, QUAD:]
    ro = _mm3(a_rb, jnp.concatenate([stack(a_p), stack(tv)], axis=1))
    r_p = r_bar + ro[:, :QUAD]
    o_loc = ro[:, QUAD:] + _mm3(a_rk, v_stack)

    bk_t = jnp.concatenate([b_hat.T, k_hat.T], axis=1)
    rowk = lax.broadcasted_iota(jnp.int32, (QUAD, QUAD), 0)
    colk = lax.broadcasted_iota(jnp.int32, (QUAD, QUAD), 1)
    same_head = (rowk // C_HEAD) == (colk // C_HEAD)
    zq = jnp.zeros((QUAD, QUAD), F32)
    m_full = (jnp.where(rowk == colk, jnp.broadcast_to(d_end, (QUAD, QUAD)), zq)
              + jnp.where(same_head, _mm3(bk_t[:, :c], a_p), zq))
    n_full = jnp.where(same_head, _mm3(bk_t, jnp.concatenate([tv, v], axis=0)), zq)

    h0 = state_ref[q]
    o = _mm3(r_p, h0) + o_loc
    state_ref[q] = _mm3(m_full, h0) + n_full
    return o


def _wkv_kernel(first, tc, *refs):
    if first:
        (x_ref, ng_ref, wc_ref, mu_ref, w0_ref, a0_ref, wlr_ref, kk_ref, ka_ref, rk_ref,
         gng_ref, gnb_ref, wbc_ref,
         out_ref, vfirst_out_ref,
         state_ref, zlast_ref, r_s, k_s, v_s, kkn_s, b_s, lw_s, o_s, gate_s, mg_s) = refs
    else:
        (x_ref, ng_ref, wc_ref, mu_ref, w0_ref, a0_ref, wlr_ref, kk_ref, ka_ref, rk_ref,
         gng_ref, gnb_ref, wbc_ref, vfirst_ref, muv_ref, v0_ref, wv2_ref,
         out_ref,
         state_ref, zlast_ref, r_s, k_s, v_s, kkn_s, b_s, lw_s, o_s, gate_s, mg_s, zvlast_ref) = refs

    j = pl.program_id(1)

    @pl.when(j == 0)
    def _():
        state_ref[...] = jnp.zeros_like(state_ref)
        zlast_ref[...] = jnp.zeros_like(zlast_ref)
        if not first:
            zvlast_ref[...] = jnp.zeros_like(zvlast_ref)

    h = _rmsnorm(x_ref[0], ng_ref[...]).astype(BF16)
    zc = _dot(h, wc_ref[:, :C_SHIFT_WIDTH])
    gate_s[...] = _dot(h, wc_ref[:, C_SHIFT_WIDTH:C_SHIFT_WIDTH + C_WIDTH])
    mg_s[...] = _dot(h, wc_ref[:, C_SHIFT_WIDTH + C_WIDTH:C_SHIFT_WIDTH + C_WIDTH + D_MODEL])

    zs = _shift_rows(zc, zlast_ref[...])
    zlast_ref[...] = zc[tc - 1:tc, :]
    zc = zc + mu_ref[...] * (zs - zc)

    r = zc[:, :C_WIDTH]
    k = zc[:, C_WIDTH:2 * C_WIDTH]
    v = zc[:, 2 * C_WIDTH:3 * C_WIDTH]
    lo = zc[:, 3 * C_WIDTH:]
    lane = lax.broadcasted_iota(jnp.int32, lo.shape, 1)
    lo = jnp.where(lane < C_DECAY_RANK, jnp.tanh(lo), lo)
    lr = _mm3(lo, wlr_ref[...])
    lw = -0.6065306597126334 * _sigmoid(w0_ref[...] + lr[:, :C_WIDTH])
    a_sig = _sigmoid(a0_ref[...] + lr[:, C_WIDTH:])

    if first:
        vfirst_out_ref[0] = v
    else:
        zv = _dot(h, wc_ref[:, C_SHIFT_WIDTH + C_WIDTH + D_MODEL:])
        zvs = _shift_rows(zv, zvlast_ref[...])
        zvlast_ref[...] = zv[tc - 1:tc, :]
        zv = zv + muv_ref[...] * (zvs - zv)
        v_mix = _sigmoid(v0_ref[...] + _mm3(zv, wv2_ref[...]))
        v = v + (vfirst_ref[0] - v) * v_mix

    rr = lax.broadcasted_iota(jnp.int32, (C_WIDTH, C_WIDTH), 0) // C_HEAD
    cc = lax.broadcasted_iota(jnp.int32, (C_WIDTH, C_WIDTH), 1) // C_HEAD
    head_ones = jnp.where(rr == cc, 1.0, 0.0).astype(BF16)
    kk = k * kk_ref[...]
    kk = kk * lax.rsqrt(_mm_exact_rhs(kk * kk, head_ones) + L2_EPS)
    k = k * (1.0 + (a_sig - 1.0) * ka_ref[...])

    r_s[...] = r
    k_s[...] = k
    v_s[...] = v
    kkn_s[...] = kk
    b_s[...] = kk * a_sig
    lw_s[...] = lw

    c = WKV_CHUNK
    tri = (lax.broadcasted_iota(jnp.int32, (c, c), 0)
           >= lax.broadcasted_iota(jnp.int32, (c, c), 1))
    tri = jnp.where(tri, 1.0, 0.0).astype(BF16)

    def chunk_body(i, carry):
        rows = pl.ds(pl.multiple_of(i * c, c), c)
        lw_c = lw_s[rows, :]
        w_hi, w_mid, w_lo = _split3(lw_c)
        cl = _dot(tri, w_hi) + (_dot(tri, w_mid) + _dot(tri, w_lo))
        for q in range(N_QUADS):
            ln = slice(q * QUAD, (q + 1) * QUAD)
            o_s[rows, ln] = _wkv_chunk(q, r_s[rows, ln], k_s[rows, ln], v_s[rows, ln],
                                       kkn_s[rows, ln], b_s[rows, ln], lw_c[:, ln], cl[:, ln],
                                       state_ref)
        return carry

    lax.fori_loop(0, tc // c, chunk_body, 0)

    o = o_s[...]
    mu = _mm_exact_rhs(o, head_ones) * (1.0 / C_HEAD)
    oc = o - mu
    var = _mm_exact_rhs(oc * oc, head_ones) * (1.0 / C_HEAD)
    o = oc * lax.rsqrt(var + GN_EPS) * gng_ref[...] + gnb_ref[...]
    r = r_s[...]
    k = k_s[...]
    v = v_s[...]
    o = o + _mm_exact_rhs(r * k * rk_ref[...], head_ones) * v
    yc = o * _silu(gate_s[...])
    out_ref[0] = _sigmoid(mg_s[...]) * _mm(yc, wbc_ref[...])


def _wkv_call(first, tc, x, ng, wc, mu, w0, a0, wlr, kkw, kaw, rkw, gng, gnb, wbc,
              vfirst=None, muv=None, v0=None, wv2=None):
    bsz, seq, _ = x.shape
    ncol = wc.shape[1]
    row = lambda w: pl.BlockSpec((1, w), lambda b, j: (0, 0))
    full = lambda s: pl.BlockSpec(s, lambda b, j: (0,) * len(s))
    tile = lambda w: pl.BlockSpec((1, tc, w), lambda b, j: (b, j, 0))
    in_specs = [tile(D_MODEL), row(D_MODEL), full((D_MODEL, ncol)), row(C_SHIFT_WIDTH),
                row(C_WIDTH), row(C_WIDTH), full((LANES, 2 * C_WIDTH)),
                row(C_WIDTH), row(C_WIDTH), row(C_WIDTH), row(C_WIDTH), row(C_WIDTH),
                full((C_WIDTH, D_MODEL))]
    args = [x, ng, wc, mu, w0, a0, wlr, kkw, kaw, rkw, gng, gnb, wbc]
    out_shape = [jax.ShapeDtypeStruct((bsz, seq, D_MODEL), F32)]
    out_specs = [tile(D_MODEL)]
    scratch = [pltpu.VMEM((N_QUADS, QUAD, QUAD), F32),
               pltpu.VMEM((1, C_SHIFT_WIDTH), F32)]
    scratch += [pltpu.VMEM((tc, C_WIDTH), F32) for _ in range(8)]
    scratch += [pltpu.VMEM((tc, D_MODEL), F32)]
    if first:
        out_shape.append(jax.ShapeDtypeStruct((bsz, seq, C_WIDTH), F32))
        out_specs.append(tile(C_WIDTH))
    else:
        in_specs += [tile(C_WIDTH), row(LANES), row(C_WIDTH), full((LANES, C_WIDTH))]
        args += [vfirst, muv, v0, wv2]
        scratch.append(pltpu.VMEM((1, LANES), F32))
    return pl.pallas_call(
        functools.partial(_wkv_kernel, first, tc),
        grid=(bsz, seq // tc),
        in_specs=in_specs,
        out_specs=out_specs,
        out_shape=out_shape,
        scratch_shapes=scratch,
        compiler_params=pltpu.CompilerParams(
            dimension_semantics=("arbitrary", "arbitrary"), vmem_limit_bytes=VMEM_LIMIT),
        name="wkv_first" if first else "wkv",
    )(*args)


def _abx_kernel(last, tm, x_ref, mc_ref, ng_ref, w_ref, alg_ref, alb_ref, aws_ref, abs_ref,
                bw_ref, bb_ref, blg_ref, blb_ref, ks_ref, vs_ref, wba_ref, wbb_ref, wbx_ref,
                wout_ref, fg_ref, out_ref, hist_ref):
    j = pl.program_id(1)

    @pl.when(j == 0)
    def _():
        hist_ref[0:CONV_HIST, :] = jnp.zeros((CONV_HIST, B_WIDTH), F32)

    x = x_ref[0]
    h = _rmsnorm(x, ng_ref[...]).astype(BF16)
    col = 0

    def proj(width):
        nonlocal col
        z = _dot(h, w_ref[:, col:col + width])
        col += width
        return z

    u = _gelu_tanh(proj(A_WIDTH))
    vv = _layernorm(_gelu_tanh(proj(A_WIDTH)), alg_ref[...], alb_ref[...], LN_EPS)
    gate_a = _silu(proj(A_WIDTH))
    gw = A_WIDTH // A_GROUPS
    causal = (lax.broadcasted_iota(jnp.int32, (A_CHUNK, A_CHUNK), 0)
              >= lax.broadcasted_iota(jnp.int32, (A_CHUNK, A_CHUNK), 1))
    ws = [jnp.where(causal, aws_ref[g], 0.0).astype(BF16) for g in range(A_GROUPS)]
    vb = vv.astype(BF16)
    s_rows = []
    for n in range(tm // A_CHUNK):
        blk = vb[n * A_CHUNK:(n + 1) * A_CHUNK, :]
        s_rows.append(jnp.concatenate(
            [_dot(ws[g], blk[:, g * gw:(g + 1) * gw]) for g in range(A_GROUPS)], axis=1) + abs_ref[...])
    s = jnp.concatenate(s_rows, axis=0)
    ya = u * s * gate_a

    hb = proj(B_WIDTH)
    hb = hb * _sigmoid(proj(B_WIDTH))
    gate_b = _silu(proj(B_WIDTH))
    hist_ref[CONV_HIST:CONV_HIST + tm, :] = hb
    off = CONV_HIST - (B_KERNEL - 1)
    acc = jnp.zeros((tm, B_WIDTH), F32) + bb_ref[...]
    for t in range(B_KERNEL):
        acc = acc + bw_ref[t:t + 1, :] * hist_ref[off + t:off + t + tm, :]
    hist_ref[0:CONV_HIST, :] = hist_ref[tm:tm + CONV_HIST, :]
    yb = _silu(_layernorm(acc, blg_ref[...], blb_ref[...], LN_EPS)) * gate_b

    qx = proj(X_WIDTH)
    sc = _mm_nt(qx, ks_ref[0, 0]) * (X_HEAD ** -0.5)
    ps = []
    for hd in range(X_HEADS):
        sh = sc[:, hd * MEM_LEN:(hd + 1) * MEM_LEN]
        e = jnp.exp(sh - jnp.max(sh, axis=-1, keepdims=True))
        ps.append(e / jnp.sum(e, axis=-1, keepdims=True))
    yx = _mm(jnp.concatenate(ps, axis=1), vs_ref[0, 0])

    merged = _sigmoid(proj(D_MODEL)) * _mm(ya, wba_ref[...])
    merged = merged + _sigmoid(proj(D_MODEL)) * _mm(yb, wbb_ref[...])
    merged = merged + _sigmoid(proj(D_MODEL)) * _mm(yx, wbx_ref[...])
    merged = merged + mc_ref[0]
    y = x + _mm(merged, wout_ref[...])
    if last:
        y = _rmsnorm(y, fg_ref[...])
    out_ref[0] = y


def _abx_call(last, tm, layer, x, mc, ng, w, alg, alb, aws, abs_full, bw, bb, blg, blb,
              kst, vst, wba, wbb, wbx, wout, fg):
    bsz, seq, _ = x.shape
    ncol = w.shape[1]
    row = lambda wd: pl.BlockSpec((1, wd), lambda b, j: (0, 0))
    full = lambda s: pl.BlockSpec(s, lambda b, j: (0,) * len(s))
    tile = lambda wd: pl.BlockSpec((1, tm, wd), lambda b, j: (b, j, 0))
    mem = pl.BlockSpec((1, 1, X_HEADS * MEM_LEN, X_WIDTH), lambda b, j: (layer, b, 0, 0))
    in_specs = [tile(D_MODEL), tile(D_MODEL), row(D_MODEL), full((D_MODEL, ncol)),
                row(A_WIDTH), row(A_WIDTH), full((A_GROUPS, A_CHUNK, A_CHUNK)), full((A_CHUNK, A_WIDTH)),
                full((CONV_HIST, B_WIDTH)), row(B_WIDTH), row(B_WIDTH), row(B_WIDTH),
                mem, mem,
                full((A_WIDTH, D_MODEL)), full((B_WIDTH, D_MODEL)), full((X_WIDTH, D_MODEL)),
                full((D_MODEL, D_MODEL)), row(D_MODEL)]
    return pl.pallas_call(
        functools.partial(_abx_kernel, last, tm),
        grid=(bsz, seq // tm),
        in_specs=in_specs,
        out_specs=tile(D_MODEL),
        out_shape=jax.ShapeDtypeStruct((bsz, seq, D_MODEL), F32),
        scratch_shapes=[pltpu.VMEM((CONV_HIST + tm, B_WIDTH), F32)],
        compiler_params=pltpu.CompilerParams(
            dimension_semantics=("arbitrary", "arbitrary"), vmem_limit_bytes=VMEM_LIMIT),
        name="abx_last" if last else "abx",
    )(x, mc, ng, w, alg, alb, aws, abs_full, bw, bb, blg, blb, kst, vst, wba, wbb, wbx, wout, fg)


def _pick_tile(seq, want, multiple):
    t = min(want, seq)
    while seq % t or t % multiple:
        t -= multiple
    return t


def kernel(x, mem, norm_g, w_in, w_in_vres, a_ln_g, a_ln_b, a_ws, a_bs, b_wdw, b_bdw, b_ln_g, b_ln_b, c_mu, c_mu_vres, c_w0, c_ww2, c_a0, c_wa2, c_v0, c_wv2, c_kk, c_ka, c_rk, c_gn_g, c_gn_b, mem_norm_g, w_mem_kv, w_branch_a, w_branch_b, w_branch_c, w_branch_x, w_out, final_norm_g):
    depth = w_in.shape[0]
    seq = x.shape[1]
    tc = _pick_tile(seq, 256, WKV_CHUNK)
    tm = _pick_tile(seq, 256, A_CHUNK)

    o_a = 0
    o_b = o_a + 3 * A_WIDTH
    o_c = o_b + 3 * B_WIDTH
    o_cg = o_c + C_SHIFT_WIDTH
    o_x = o_cg + C_WIDTH
    o_m = o_x + X_WIDTH

    kst, vst = _memkv(mem, mem_norm_g, w_mem_kv)
    zpad = lambda a, n, axis: jnp.pad(a, [(0, n - a.shape[i]) if i == axis else (0, 0) for i in range(a.ndim)])
    r1 = lambda a: a.reshape(1, -1)
    v_first = None
    for i in range(depth):
        wi = w_in[i]
        first = i == 0
        cols = [wi[:, o_c:o_cg], wi[:, o_cg:o_x], wi[:, o_m + 2 * D_MODEL:o_m + 3 * D_MODEL]]
        if not first:
            cols.append(zpad(w_in_vres[i - 1], LANES, 1))
        wc = jnp.concatenate(cols, axis=1).astype(BF16)
        wlr = jnp.concatenate([
            jnp.concatenate([c_ww2[i], jnp.zeros_like(c_ww2[i])], axis=1),
            jnp.concatenate([jnp.zeros_like(c_wa2[i]), c_wa2[i]], axis=1)], axis=0)
        common = (x, r1(norm_g[i]), wc, r1(c_mu[i]), r1(c_w0[i]), r1(c_a0[i]), wlr,
                  r1(c_kk[i]), r1(c_ka[i]), r1(c_rk[i]), r1(c_gn_g[i]), r1(c_gn_b[i]),
                  w_branch_c[i].astype(BF16))
        if first:
            mc, v_first = _wkv_call(True, tc, *common)
        else:
            (mc,) = _wkv_call(False, tc, *common, v_first, zpad(r1(c_mu_vres[i - 1]), LANES, 1),
                              r1(c_v0[i - 1]), zpad(c_wv2[i - 1], LANES, 0))
        wabx = jnp.concatenate([wi[:, o_a:o_c], wi[:, o_x:o_m], wi[:, o_m:o_m + 2 * D_MODEL],
                                wi[:, o_m + 3 * D_MODEL:o_m + 4 * D_MODEL]], axis=1).astype(BF16)
        abs_full = jnp.repeat(a_bs[i].T, A_WIDTH // A_GROUPS, axis=1)
        x = _abx_call(i == depth - 1, tm, i, x, mc, r1(norm_g[i]), wabx,
                      r1(a_ln_g[i]), r1(a_ln_b[i]), a_ws[i], abs_full,
                      zpad(b_wdw[i], CONV_HIST, 0), r1(b_bdw[i]), r1(b_ln_g[i]), r1(b_ln_b[i]),
                      kst, vst, w_branch_a[i].astype(BF16), w_branch_b[i].astype(BF16),
                      w_branch_x[i].astype(BF16), w_out[i].astype(BF16), r1(final_norm_g))
    return x
```

```python
import functools

import jax
import jax.numpy as jnp
from jax import lax
from jax.experimental import pallas as pl
from jax.experimental.pallas import tpu as pltpu

F32 = jnp.float32
BF16 = jnp.bfloat16

D_MODEL = 1024
MEM_LEN = 256
A_WIDTH = 512
A_GROUPS = 4
A_CHUNK = 128
B_WIDTH = 512
B_KERNEL = 31
C_HEAD = 64
C_HEADS = 8
C_WIDTH = C_HEADS * C_HEAD
C_DECAY_RANK = 64
C_AAA_RANK = 64
C_VRES_RANK = 32
C_SHIFT_WIDTH = 3 * C_WIDTH + C_DECAY_RANK + C_AAA_RANK
X_HEADS = 4
X_HEAD = 64
X_WIDTH = X_HEADS * X_HEAD

RMS_EPS = 1e-6
LN_EPS = 1e-5
GN_EPS = 64e-5
L2_EPS = 1e-12

LANES = 128
WKV_CHUNK = 64
WKV_MID = WKV_CHUNK // 2 - 1
GROUP = 4 * C_HEAD
N_GROUPS = C_WIDTH // GROUP
CONV_HIST = 32
VMEM_LIMIT = 56 * 1024 * 1024


def _split2(x):
    hi = x.astype(BF16)
    lo = (x - hi.astype(F32)).astype(BF16)
    return hi, lo


def _split3(x):
    hi = x.astype(BF16)
    r1 = x - hi.astype(F32)
    mid = r1.astype(BF16)
    lo = (r1 - mid.astype(F32)).astype(BF16)
    return hi, mid, lo


def _dot(a, b):
    return jnp.dot(a, b, preferred_element_type=F32)


def _mm(a, b):
    return _dot(a.astype(BF16), b.astype(BF16))


def _mm_nt(a, b):
    return lax.dot_general(a.astype(BF16), b.astype(BF16), (((1,), (1,)), ((), ())),
                           preferred_element_type=F32)


def _mm3(a, b):
    ah, al = _split2(a)
    bh, bl = _split2(b)
    return _dot(ah, bh) + (_dot(ah, bl) + _dot(al, bh))


def _mm_exact_rhs(a, b_bf16):
    ah, al = _split2(a)
    return _dot(ah, b_bf16) + _dot(al, b_bf16)


def _rmsnorm(x, g):
    ms = jnp.mean(x * x, axis=-1, keepdims=True)
    return x * lax.rsqrt(ms + RMS_EPS) * g


def _layernorm(x, g, b, eps):
    mu = jnp.mean(x, axis=-1, keepdims=True)
    xc = x - mu
    var = jnp.mean(xc * xc, axis=-1, keepdims=True)
    return xc * lax.rsqrt(var + eps) * g + b


def _sigmoid(x):
    return jax.nn.sigmoid(x)


def _silu(x):
    return x * jax.nn.sigmoid(x)


def _gelu_tanh(x):
    c = 0.7978845608028654
    return 0.5 * x * (1.0 + jnp.tanh(c * (x + 0.044715 * (x * x * x))))


def _head_stack(x, n_heads, head_width):
    lane_head = lax.broadcasted_iota(jnp.int32, x.shape, 1) // head_width
    zero = jnp.zeros_like(x)
    return jnp.concatenate([jnp.where(lane_head == h, x, zero) for h in range(n_heads)], axis=0)


def _shift_rows(z, prev_row):
    rolled = pltpu.roll(z, 1, 0)
    row = lax.broadcasted_iota(jnp.int32, z.shape, 0)
    return jnp.where(row == 0, prev_row, rolled)


def _memkv_kernel(mem_ref, g_ref, w_ref, k_ref, v_ref):
    mem_n = _rmsnorm(mem_ref[0], g_ref[...])
    kv = _mm(mem_n, w_ref[0])
    k_ref[0, 0] = _head_stack(kv[:, :X_WIDTH], X_HEADS, X_HEAD).astype(BF16)
    v_ref[0, 0] = _head_stack(kv[:, X_WIDTH:], X_HEADS, X_HEAD).astype(BF16)


def _memkv(mem, mem_norm_g, w_mem_kv):
    bsz = mem.shape[0]
    depth = w_mem_kv.shape[0]
    out = jax.ShapeDtypeStruct((depth, bsz, X_HEADS * MEM_LEN, X_WIDTH), BF16)
    spec = pl.BlockSpec((1, 1, X_HEADS * MEM_LEN, X_WIDTH), lambda l, b: (l, b, 0, 0))
    return pl.pallas_call(
        _memkv_kernel,
        grid=(depth, bsz),
        in_specs=[pl.BlockSpec((1, MEM_LEN, D_MODEL), lambda l, b: (b, 0, 0)),
                  pl.BlockSpec((1, D_MODEL), lambda l, b: (0, 0)),
                  pl.BlockSpec((1, D_MODEL, 2 * X_WIDTH), lambda l, b: (l, 0, 0))],
        out_specs=[spec, spec],
        out_shape=[out, out],
        name="memkv",
    )(mem, mem_norm_g.reshape(1, D_MODEL), w_mem_kv)


def _wkv_local(chains):
    c = WKV_CHUNK
    gw = GROUP
    nh = gw // C_HEAD
    stack = lambda z: _head_stack(z.astype(BF16), nh, C_HEAD)
    cat0 = lambda *zs: jnp.concatenate(zs, axis=0)
    cat1 = lambda *zs: jnp.concatenate(zs, axis=1)
    row = lax.broadcasted_iota(jnp.int32, (c, gw), 0)
    col = lax.broadcasted_iota(jnp.int32, (c, gw), 1) % c
    strict = row > col
    incl = row >= col
    eye = jnp.where(row == col, 1.0, 0.0)
    zero = jnp.zeros((c, gw), F32)
    rowk = lax.broadcasted_iota(jnp.int32, (gw, gw), 0)
    colk = lax.broadcasted_iota(jnp.int32, (gw, gw), 1)
    same_head = (rowk // C_HEAD) == (colk // C_HEAD)
    diag = rowk == colk
    zq = jnp.zeros((gw, gw), F32)

    prep = []
    for r, k, v, kk, b, lw, cl in chains:
        m = cl[WKV_MID:WKV_MID + 1, :]
        e_fwd = jnp.exp(cl - m)
        e_inv = jnp.exp(m - cl)
        e_mid = jnp.exp(m)
        e_end = jnp.exp(cl[c - 1:c, :] - m)
        r_t = r * e_fwd
        a_t = -kk * jnp.exp(cl - lw - m)
        b_t = b * e_inv
        k_t = k * e_inv
        prep.append(dict(
            v=v, r_t=r_t, a_t=a_t, b_t=b_t, k_t=k_t,
            r_bar=r_t * e_mid,
            a_bar=a_t * e_mid,
            bk_hat=cat1((b_t * e_end).T, (k_t * e_end).T).astype(BF16),
            d_end=e_end * e_mid))

    g = [_mm_nt(cat0(p["a_t"], p["r_t"]), cat0(stack(p["b_t"]), stack(p["k_t"]))) for p in prep]
    l_ab = [jnp.where(strict, x[:c, :gw], zero) for x in g]
    l_ak = [jnp.where(strict, x[:c, gw:], zero) for x in g]
    a_rb = [jnp.where(incl, x[c:, :gw], zero) for x in g]
    a_rk = [jnp.where(incl, x[c:, gw:], zero) for x in g]

    p_acc = [eye + x for x in l_ab]
    q_pow = [_mm(x, stack(x)) for x in l_ab]
    for _ in range(4):
        step = [_mm(cat0(p, q), stack(q)) for p, q in zip(p_acc, q_pow)]
        p_acc = [p + x[:c] for p, x in zip(p_acc, step)]
        q_pow = [x[c:] for x in step]
    t_inv = [p + _mm(p, stack(q)) for p, q in zip(p_acc, q_pow)]

    lv = [_mm(cat0(x, y), stack(p["v"])) for x, y, p in zip(l_ak, a_rk, prep)]
    tva = [_mm(t, cat1(stack(x[:c]), stack(p["a_bar"]))) for t, x, p in zip(t_inv, lv, prep)]
    tv = [x[:, :gw] for x in tva]
    a_p = [x[:, gw:] for x in tva]
    ro = [_mm(x, cat1(stack(y), stack(z))) for x, y, z in zip(a_rb, a_p, tv)]
    r_p = [(p["r_bar"] + x[:, :gw]).astype(BF16) for p, x in zip(prep, ro)]
    o_loc = [x[:, gw:] + y[c:] for x, y in zip(ro, lv)]
    m_full = [(jnp.where(diag, jnp.broadcast_to(p["d_end"], (gw, gw)), zq)
               + jnp.where(same_head, _mm(p["bk_hat"][:, :c], x), zq)).astype(BF16)
              for p, x in zip(prep, a_p)]
    n_full = [jnp.where(same_head, _mm(p["bk_hat"], cat0(x, p["v"])), zq) for p, x in zip(prep, tv)]
    return list(zip(r_p, o_loc, m_full, n_full))


def _wkv_kernel(first, tc, *refs):
    if first:
        (x_ref, ng_ref, wc_ref, mu_ref, w0_ref, a0_ref, wlr_ref, kk_ref, ka_ref, rk_ref,
         gng_ref, gnb_ref, wbc_ref,
         out_ref, vfirst_out_ref,
         state_ref, zlast_ref, o_s, gate_s, mg_s) = refs
    else:
        (x_ref, ng_ref, wc_ref, mu_ref, w0_ref, a0_ref, wlr_ref, kk_ref, ka_ref, rk_ref,
         gng_ref, gnb_ref, wbc_ref, vfirst_ref, muv_ref, v0_ref, wv2_ref,
         out_ref,
         state_ref, zlast_ref, o_s, gate_s, mg_s, zvlast_ref) = refs

    j = pl.program_id(1)

    @pl.when(j == 0)
    def _():
        state_ref[...] = jnp.zeros_like(state_ref)
        zlast_ref[...] = jnp.zeros_like(zlast_ref)
        if not first:
            zvlast_ref[...] = jnp.zeros_like(zvlast_ref)

    h = _rmsnorm(x_ref[0], ng_ref[...]).astype(BF16)
    zc = _dot(h, wc_ref[:, :C_SHIFT_WIDTH])
    gate_s[...] = _dot(h, wc_ref[:, C_SHIFT_WIDTH:C_SHIFT_WIDTH + C_WIDTH])
    mg_s[...] = _dot(h, wc_ref[:, C_SHIFT_WIDTH + C_WIDTH:C_SHIFT_WIDTH + C_WIDTH + D_MODEL])

    zs = _shift_rows(zc, zlast_ref[...])
    zlast_ref[...] = zc[tc - 1:tc, :]
    zc = zc + mu_ref[...] * (zs - zc)

    r = zc[:, :C_WIDTH]
    k = zc[:, C_WIDTH:2 * C_WIDTH]
    v = zc[:, 2 * C_WIDTH:3 * C_WIDTH]
    lo = zc[:, 3 * C_WIDTH:]
    lane = lax.broadcasted_iota(jnp.int32, lo.shape, 1)
    lo = jnp.where(lane < C_DECAY_RANK, jnp.tanh(lo), lo)
    lr = _mm3(lo, wlr_ref[...])
    lw = -0.6065306597126334 * _sigmoid(w0_ref[...] + lr[:, :C_WIDTH])
    a_sig = _sigmoid(a0_ref[...] + lr[:, C_WIDTH:])

    if first:
        vfirst_out_ref[0] = v
    else:
        zv = _dot(h, wc_ref[:, C_SHIFT_WIDTH + C_WIDTH + D_MODEL:])
        zvs = _shift_rows(zv, zvlast_ref[...])
        zvlast_ref[...] = zv[tc - 1:tc, :]
        zv = zv + muv_ref[...] * (zvs - zv)
        v_mix = _sigmoid(v0_ref[...] + _mm3(zv, wv2_ref[...]))
        v = v + (vfirst_ref[0] - v) * v_mix

    rr = lax.broadcasted_iota(jnp.int32, (C_WIDTH, C_WIDTH), 0) // C_HEAD
    cc = lax.broadcasted_iota(jnp.int32, (C_WIDTH, C_WIDTH), 1) // C_HEAD
    head_ones = jnp.where(rr == cc, 1.0, 0.0).astype(BF16)
    kk = k * kk_ref[...]
    kk = kk * lax.rsqrt(_mm_exact_rhs(kk * kk, head_ones) + L2_EPS)
    k = k * (1.0 + (a_sig - 1.0) * ka_ref[...])

    c = WKV_CHUNK
    n_chunks = tc // c
    blk = (lax.broadcasted_iota(jnp.int32, (tc, tc), 0) // c
           == lax.broadcasted_iota(jnp.int32, (tc, tc), 1) // c)
    low = (lax.broadcasted_iota(jnp.int32, (tc, tc), 0)
           >= lax.broadcasted_iota(jnp.int32, (tc, tc), 1))
    tri = jnp.where(blk & low, 1.0, 0.0).astype(BF16)
    w_hi, w_mid, w_lo = _split3(lw)
    cl = _dot(tri, w_hi) + (_dot(tri, w_mid) + _dot(tri, w_lo))
    b_all = kk * a_sig
    keys = [(i, q) for i in range(n_chunks) for q in range(N_GROUPS)]
    sl = lambda z, i, q: z[i * c:(i + 1) * c, q * GROUP:(q + 1) * GROUP]
    local = dict(zip(keys, _wkv_local(
        [tuple(sl(z, i, q) for z in (r, k, v, kk, b_all, lw, cl)) for i, q in keys])))
    hs = [state_ref[q] for q in range(N_GROUPS)]
    for i in range(n_chunks):
        for q in range(N_GROUPS):
            r_p, o_loc, m_full, n_full = local[i, q]
            both = _dot(jnp.concatenate([r_p, m_full], axis=0), hs[q].astype(BF16))
            o_s[i * c:(i + 1) * c, q * GROUP:(q + 1) * GROUP] = both[:c] + o_loc
            hs[q] = both[c:] + n_full
    for q in range(N_GROUPS):
        state_ref[q] = hs[q]

    o = o_s[...]
    mu = _mm_exact_rhs(o, head_ones) * (1.0 / C_HEAD)
    oc = o - mu
    var = _mm_exact_rhs(oc * oc, head_ones) * (1.0 / C_HEAD)
    o = oc * lax.rsqrt(var + GN_EPS) * gng_ref[...] + gnb_ref[...]
    o = o + _mm_exact_rhs(r * k * rk_ref[...], head_ones) * v
    yc = o * _silu(gate_s[...])
    out_ref[0] = _sigmoid(mg_s[...]) * _mm(yc, wbc_ref[...])


def _wkv_call(first, tc, x, ng, wc, mu, w0, a0, wlr, kkw, kaw, rkw, gng, gnb, wbc,
              vfirst=None, muv=None, v0=None, wv2=None):
    bsz, seq, _ = x.shape
    ncol = wc.shape[1]
    row = lambda w: pl.BlockSpec((1, w), lambda b, j: (0, 0))
    full = lambda s: pl.BlockSpec(s, lambda b, j: (0,) * len(s))
    tile = lambda w: pl.BlockSpec((1, tc, w), lambda b, j: (b, j, 0))
    in_specs = [tile(D_MODEL), row(D_MODEL), full((D_MODEL, ncol)), row(C_SHIFT_WIDTH),
                row(C_WIDTH), row(C_WIDTH), full((LANES, 2 * C_WIDTH)),
                row(C_WIDTH), row(C_WIDTH), row(C_WIDTH), row(C_WIDTH), row(C_WIDTH),
                full((C_WIDTH, D_MODEL))]
    args = [x, ng, wc, mu, w0, a0, wlr, kkw, kaw, rkw, gng, gnb, wbc]
    out_shape = [jax.ShapeDtypeStruct((bsz, seq, D_MODEL), F32)]
    out_specs = [tile(D_MODEL)]
    scratch = [pltpu.VMEM((N_GROUPS, GROUP, GROUP), F32),
               pltpu.VMEM((1, C_SHIFT_WIDTH), F32)]
    scratch += [pltpu.VMEM((tc, C_WIDTH), F32) for _ in range(2)]
    scratch += [pltpu.VMEM((tc, D_MODEL), F32)]
    if first:
        out_shape.append(jax.ShapeDtypeStruct((bsz, seq, C_WIDTH), F32))
        out_specs.append(tile(C_WIDTH))
    else:
        in_specs += [tile(C_WIDTH), row(LANES), row(C_WIDTH), full((LANES, C_WIDTH))]
        args += [vfirst, muv, v0, wv2]
        scratch.append(pltpu.VMEM((1, LANES), F32))
    return pl.pallas_call(
        functools.partial(_wkv_kernel, first, tc),
        grid=(bsz, seq // tc),
        in_specs=in_specs,
        out_specs=out_specs,
        out_shape=out_shape,
        scratch_shapes=scratch,
        compiler_params=pltpu.CompilerParams(
            dimension_semantics=("arbitrary", "arbitrary"), vmem_limit_bytes=VMEM_LIMIT),
        name="wkv_first" if first else "wkv",
    )(*args)


def _abx_kernel(last, tm, x_ref, mc_ref, ng_ref, w_ref, alg_ref, alb_ref, aws_ref, abs_ref,
                bw_ref, bb_ref, blg_ref, blb_ref, ks_ref, vs_ref, wba_ref, wbb_ref, wbx_ref,
                wout_ref, fg_ref, out_ref, hist_ref):
    j = pl.program_id(1)

    @pl.when(j == 0)
    def _():
        hist_ref[0:CONV_HIST, :] = jnp.zeros((CONV_HIST, B_WIDTH), F32)

    x = x_ref[0]
    h = _rmsnorm(x, ng_ref[...]).astype(BF16)
    col = 0

    def proj(width):
        nonlocal col
        z = _dot(h, w_ref[:, col:col + width])
        col += width
        return z

    u = _gelu_tanh(proj(A_WIDTH))
    vv = _layernorm(_gelu_tanh(proj(A_WIDTH)), alg_ref[...], alb_ref[...], LN_EPS)
    gate_a = _silu(proj(A_WIDTH))
    gw = A_WIDTH // A_GROUPS
    causal = (lax.broadcasted_iota(jnp.int32, (A_CHUNK, A_CHUNK), 0)
              >= lax.broadcasted_iota(jnp.int32, (A_CHUNK, A_CHUNK), 1))
    ws = [jnp.where(causal, aws_ref[g], 0.0).astype(BF16) for g in range(A_GROUPS)]
    vb = vv.astype(BF16)
    s_rows = []
    for n in range(tm // A_CHUNK):
        blk = vb[n * A_CHUNK:(n + 1) * A_CHUNK, :]
        s_rows.append(jnp.concatenate(
            [_dot(ws[g], blk[:, g * gw:(g + 1) * gw]) for g in range(A_GROUPS)], axis=1) + abs_ref[...])
    s = jnp.concatenate(s_rows, axis=0)
    ya = u * s * gate_a

    hb = proj(B_WIDTH)
    hb = hb * _sigmoid(proj(B_WIDTH))
    gate_b = _silu(proj(B_WIDTH))
    hist_ref[CONV_HIST:CONV_HIST + tm, :] = hb
    off = CONV_HIST - (B_KERNEL - 1)
    acc = jnp.zeros((tm, B_WIDTH), F32) + bb_ref[...]
    for t in range(B_KERNEL):
        acc = acc + bw_ref[t:t + 1, :] * hist_ref[off + t:off + t + tm, :]
    hist_ref[0:CONV_HIST, :] = hist_ref[tm:tm + CONV_HIST, :]
    yb = _silu(_layernorm(acc, blg_ref[...], blb_ref[...], LN_EPS)) * gate_b

    qx = proj(X_WIDTH)
    sc = _mm_nt(qx, ks_ref[0, 0]) * (X_HEAD ** -0.5)
    ps = []
    for hd in range(X_HEADS):
        sh = sc[:, hd * MEM_LEN:(hd + 1) * MEM_LEN]
        e = jnp.exp(sh - jnp.max(sh, axis=-1, keepdims=True))
        ps.append(e / jnp.sum(e, axis=-1, keepdims=True))
    yx = _mm(jnp.concatenate(ps, axis=1), vs_ref[0, 0])

    merged = _sigmoid(proj(D_MODEL)) * _mm(ya, wba_ref[...])
    merged = merged + _sigmoid(proj(D_MODEL)) * _mm(yb, wbb_ref[...])
    merged = merged + _sigmoid(proj(D_MODEL)) * _mm(yx, wbx_ref[...])
    merged = merged + mc_ref[0]
    y = x + _mm(merged, wout_ref[...])
    if last:
        y = _rmsnorm(y, fg_ref[...])
    out_ref[0] = y


def _abx_call(last, tm, layer, x, mc, ng, w, alg, alb, aws, abs_full, bw, bb, blg, blb,
              kst, vst, wba, wbb, wbx, wout, fg):
    bsz, seq, _ = x.shape
    ncol = w.shape[1]
    row = lambda wd: pl.BlockSpec((1, wd), lambda b, j: (0, 0))
    full = lambda s: pl.BlockSpec(s, lambda b, j: (0,) * len(s))
    tile = lambda wd: pl.BlockSpec((1, tm, wd), lambda b, j: (b, j, 0))
    mem = pl.BlockSpec((1, 1, X_HEADS * MEM_LEN, X_WIDTH), lambda b, j: (layer, b, 0, 0))
    in_specs = [tile(D_MODEL), tile(D_MODEL), row(D_MODEL), full((D_MODEL, ncol)),
                row(A_WIDTH), row(A_WIDTH), full((A_GROUPS, A_CHUNK, A_CHUNK)), full((A_CHUNK, A_WIDTH)),
                full((CONV_HIST, B_WIDTH)), row(B_WIDTH), row(B_WIDTH), row(B_WIDTH),
                mem, mem,
                full((A_WIDTH, D_MODEL)), full((B_WIDTH, D_MODEL)), full((X_WIDTH, D_MODEL)),
                full((D_MODEL, D_MODEL)), row(D_MODEL)]
    return pl.pallas_call(
        functools.partial(_abx_kernel, last, tm),
        grid=(bsz, seq // tm),
        in_specs=in_specs,
        out_specs=tile(D_MODEL),
        out_shape=jax.ShapeDtypeStruct((bsz, seq, D_MODEL), F32),
        scratch_shapes=[pltpu.VMEM((CONV_HIST + tm, B_WIDTH), F32)],
        compiler_params=pltpu.CompilerParams(
            dimension_semantics=("arbitrary", "arbitrary"), vmem_limit_bytes=VMEM_LIMIT),
        name="abx_last" if last else "abx",
    )(x, mc, ng, w, alg, alb, aws, abs_full, bw, bb, blg, blb, kst, vst, wba, wbb, wbx, wout, fg)


def _pick_tile(seq, want, multiple):
    t = min(want, seq)
    while seq % t or t % multiple:
        t -= multiple
    return t


def kernel(x, mem, norm_g, w_in, w_in_vres, a_ln_g, a_ln_b, a_ws, a_bs, b_wdw, b_bdw, b_ln_g, b_ln_b, c_mu, c_mu_vres, c_w0, c_ww2, c_a0, c_wa2, c_v0, c_wv2, c_kk, c_ka, c_rk, c_gn_g, c_gn_b, mem_norm_g, w_mem_kv, w_branch_a, w_branch_b, w_branch_c, w_branch_x, w_out, final_norm_g):
    depth = w_in.shape[0]
    seq = x.shape[1]
    tc = _pick_tile(seq, 512, WKV_CHUNK)
    tm = _pick_tile(seq, 256, A_CHUNK)

    o_a = 0
    o_b = o_a + 3 * A_WIDTH
    o_c = o_b + 3 * B_WIDTH
    o_cg = o_c + C_SHIFT_WIDTH
    o_x = o_cg + C_WIDTH
    o_m = o_x + X_WIDTH

    kst, vst = _memkv(mem, mem_norm_g, w_mem_kv)
    zpad = lambda a, n, axis: jnp.pad(a, [(0, n - a.shape[i]) if i == axis else (0, 0) for i in range(a.ndim)])
    r1 = lambda a: a.reshape(1, -1)
    v_first = None
    for i in range(depth):
        wi = w_in[i]
        first = i == 0
        cols = [wi[:, o_c:o_cg], wi[:, o_cg:o_x], wi[:, o_m + 2 * D_MODEL:o_m + 3 * D_MODEL]]
        if not first:
            cols.append(zpad(w_in_vres[i - 1], LANES, 1))
        wc = jnp.concatenate(cols, axis=1).astype(BF16)
        wlr = jnp.concatenate([
            jnp.concatenate([c_ww2[i], jnp.zeros_like(c_ww2[i])], axis=1),
            jnp.concatenate([jnp.zeros_like(c_wa2[i]), c_wa2[i]], axis=1)], axis=0)
        common = (x, r1(norm_g[i]), wc, r1(c_mu[i]), r1(c_w0[i]), r1(c_a0[i]), wlr,
                  r1(c_kk[i]), r1(c_ka[i]), r1(c_rk[i]), r1(c_gn_g[i]), r1(c_gn_b[i]),
                  w_branch_c[i].astype(BF16))
        if first:
            mc, v_first = _wkv_call(True, tc, *common)
        else:
            (mc,) = _wkv_call(False, tc, *common, v_first, zpad(r1(c_mu_vres[i - 1]), LANES, 1),
                              r1(c_v0[i - 1]), zpad(c_wv2[i - 1], LANES, 0))
        wabx = jnp.concatenate([wi[:, o_a:o_c], wi[:, o_x:o_m], wi[:, o_m:o_m + 2 * D_MODEL],
                                wi[:, o_m + 3 * D_MODEL:o_m + 4 * D_MODEL]], axis=1).astype(BF16)
        abs_full = jnp.repeat(a_bs[i].T, A_WIDTH // A_GROUPS, axis=1)
        x = _abx_call(i == depth - 1, tm, i, x, mc, r1(norm_g[i]), wabx,
                      r1(a_ln_g[i]), r1(a_ln_b[i]), a_ws[i], abs_full,
                      zpad(b_wdw[i], CONV_HIST, 0), r1(b_bdw[i]), r1(b_ln_g[i]), r1(b_ln_b[i]),
                      kst, vst, w_branch_a[i].astype(BF16), w_branch_b[i].astype(BF16),
                      w_branch_x[i].astype(BF16), w_out[i].astype(BF16), r1(final_norm_g))
    return x
```

```python
import functools

import jax
import jax.numpy as jnp
from jax import lax
from jax.experimental import pallas as pl
from jax.experimental.pallas import tpu as pltpu

F32 = jnp.float32
BF16 = jnp.bfloat16

D_MODEL = 1024
MEM_LEN = 256
A_WIDTH = 512
A_GROUPS = 4
A_CHUNK = 128
B_WIDTH = 512
B_KERNEL = 31
C_HEAD = 64
C_HEADS = 8
C_WIDTH = C_HEADS * C_HEAD
C_DECAY_RANK = 64
C_AAA_RANK = 64
C_VRES_RANK = 32
C_SHIFT_WIDTH = 3 * C_WIDTH + C_DECAY_RANK + C_AAA_RANK
X_HEADS = 4
X_HEAD = 64
X_WIDTH = X_HEADS * X_HEAD

RMS_EPS = 1e-6
LN_EPS = 1e-5
GN_EPS = 64e-5
L2_EPS = 1e-12

LANES = 128
SUBLANES = 8
WKV_CHUNK = 64
WKV_MID = WKV_CHUNK // 2 - 1
GROUP = 2 * C_HEAD
N_GROUPS = C_WIDTH // GROUP
SHIFT_W = C_SHIFT_WIDTH + LANES
WC_COLS = SHIFT_W + C_WIDTH + D_MODEL
CONV_HIST = 32
VMEM_LIMIT = 56 * 1024 * 1024


def _split2(x):
    hi = x.astype(BF16)
    lo = (x - hi.astype(F32)).astype(BF16)
    return hi, lo


def _dot(a, b):
    return jnp.dot(a, b, preferred_element_type=F32)


def _mm(a, b):
    return _dot(a.astype(BF16), b.astype(BF16))


def _mm_nt(a, b):
    return lax.dot_general(a.astype(BF16), b.astype(BF16), (((1,), (1,)), ((), ())),
                           preferred_element_type=F32)


def _rmsnorm(x, g):
    ms = jnp.mean(x * x, axis=-1, keepdims=True)
    return x * lax.rsqrt(ms + RMS_EPS) * g


def _layernorm(x, g, b, eps):
    mu = jnp.mean(x, axis=-1, keepdims=True)
    xc = x - mu
    var = jnp.mean(xc * xc, axis=-1, keepdims=True)
    return xc * lax.rsqrt(var + eps) * g + b


def _sigmoid(x):
    return jax.nn.sigmoid(x)


def _silu(x):
    return x * jax.nn.sigmoid(x)


def _gelu_tanh(x):
    c = 0.7978845608028654
    return 0.5 * x * (1.0 + jnp.tanh(c * (x + 0.044715 * (x * x * x))))


def _head_stack(x, n_heads, head_width):
    lane_head = lax.broadcasted_iota(jnp.int32, x.shape, 1) // head_width
    zero = jnp.zeros_like(x)
    return jnp.concatenate([jnp.where(lane_head == h, x, zero) for h in range(n_heads)], axis=0)


def _shift_rows(z, prev_row):
    rolled = pltpu.roll(z, 1, 0)
    row = lax.broadcasted_iota(jnp.int32, z.shape, 0)
    return jnp.where(row == 0, prev_row, rolled)


def _memkv_kernel(mem_ref, g_ref, w_ref, k_ref, v_ref):
    mem_n = _rmsnorm(mem_ref[0], g_ref[...])
    kv = _mm(mem_n, w_ref[0])
    k_ref[0, 0] = _head_stack(kv[:, :X_WIDTH], X_HEADS, X_HEAD).astype(BF16)
    v_ref[0, 0] = _head_stack(kv[:, X_WIDTH:], X_HEADS, X_HEAD).astype(BF16)


def _memkv(mem, mem_norm_g, w_mem_kv):
    bsz = mem.shape[0]
    depth = w_mem_kv.shape[0]
    out = jax.ShapeDtypeStruct((depth, bsz, X_HEADS * MEM_LEN, X_WIDTH), BF16)
    spec = pl.BlockSpec((1, 1, X_HEADS * MEM_LEN, X_WIDTH), lambda l, b: (l, b, 0, 0))
    return pl.pallas_call(
        _memkv_kernel,
        grid=(depth, bsz),
        in_specs=[pl.BlockSpec((1, MEM_LEN, D_MODEL), lambda l, b: (b, 0, 0)),
                  pl.BlockSpec((1, D_MODEL), lambda l, b: (0, 0)),
                  pl.BlockSpec((1, D_MODEL, 2 * X_WIDTH), lambda l, b: (l, 0, 0))],
        out_specs=[spec, spec],
        out_shape=[out, out],
        name="memkv",
    )(mem, mem_norm_g.reshape(1, D_MODEL), w_mem_kv)


def _wkv_local(chains):
    c = WKV_CHUNK
    gw = GROUP
    nh = gw // C_HEAD
    stack = lambda z: _head_stack(z.astype(BF16), nh, C_HEAD)
    cat0 = lambda *zs: jnp.concatenate(zs, axis=0)
    cat1 = lambda *zs: jnp.concatenate(zs, axis=1)
    row = lax.broadcasted_iota(jnp.int32, (c, gw), 0)
    col = lax.broadcasted_iota(jnp.int32, (c, gw), 1) % c
    strict = row > col
    incl = row >= col
    eye = jnp.where(row == col, 1.0, 0.0)
    zero = jnp.zeros((c, gw), F32)
    rowk = lax.broadcasted_iota(jnp.int32, (gw, gw), 0)
    colk = lax.broadcasted_iota(jnp.int32, (gw, gw), 1)
    same_head = (rowk // C_HEAD) == (colk // C_HEAD)
    diag = rowk == colk
    zq = jnp.zeros((gw, gw), F32)

    prep = []
    for r, k, v, kk, b, lw, cl in chains:
        m = cl[WKV_MID:WKV_MID + 1, :]
        e_fwd = jnp.exp(cl - m)
        e_inv = jnp.exp(m - cl)
        e_mid = jnp.exp(m)
        e_end = jnp.exp(cl[c - 1:c, :] - m)
        r_t = r * e_fwd
        a_t = -kk * jnp.exp(cl - lw - m)
        b_t = b * e_inv
        k_t = k * e_inv
        prep.append(dict(
            v=v, r_t=r_t, a_t=a_t, b_t=b_t, k_t=k_t,
            r_bar=r_t * e_mid,
            a_bar=a_t * e_mid,
            bk_hat=cat1((b_t * e_end).T, (k_t * e_end).T).astype(BF16),
            d_end=e_end * e_mid))

    g = [_mm_nt(cat0(p["a_t"], p["r_t"]), cat0(stack(p["b_t"]), stack(p["k_t"]))) for p in prep]
    l_ab = [jnp.where(strict, x[:c, :gw], zero) for x in g]
    l_ak = [jnp.where(strict, x[:c, gw:], zero) for x in g]
    a_rb = [jnp.where(incl, x[c:, :gw], zero) for x in g]
    a_rk = [jnp.where(incl, x[c:, gw:], zero) for x in g]

    p_acc = [eye + x for x in l_ab]
    q_pow = [_mm(x, stack(x)) for x in l_ab]
    for _ in range(4):
        step = [_mm(cat0(p, q), stack(q)) for p, q in zip(p_acc, q_pow)]
        p_acc = [p + x[:c] for p, x in zip(p_acc, step)]
        q_pow = [x[c:] for x in step]
    t_inv = [p + _mm(p, stack(q)) for p, q in zip(p_acc, q_pow)]

    lv = [_mm(cat0(x, y), stack(p["v"])) for x, y, p in zip(l_ak, a_rk, prep)]
    tva = [_mm(t, cat1(stack(x[:c]), stack(p["a_bar"]))) for t, x, p in zip(t_inv, lv, prep)]
    tv = [x[:, :gw] for x in tva]
    a_p = [x[:, gw:] for x in tva]
    ro = [_mm(x, cat1(stack(y), stack(z))) for x, y, z in zip(a_rb, a_p, tv)]
    r_p = [(p["r_bar"] + x[:, :gw]).astype(BF16) for p, x in zip(prep, ro)]
    o_loc = [x[:, gw:] + y[c:] for x, y in zip(ro, lv)]
    m_full = [(jnp.where(diag, jnp.broadcast_to(p["d_end"], (gw, gw)), zq)
               + jnp.where(same_head, _mm(p["bk_hat"][:, :c], x), zq)).astype(BF16)
              for p, x in zip(prep, a_p)]
    n_full = [jnp.where(same_head, _mm(p["bk_hat"], cat0(x, p["v"])), zq) for p, x in zip(prep, tv)]
    return list(zip(r_p, o_loc, m_full, n_full))


def _wkv_kernel(first, tc, x_ref, ng_ref, wc_ref, mu_ref, w0_ref, a0_ref, wlr_ref, kk_ref, ka_ref,
                rk_ref, gng_ref, gnb_ref, wbc_ref, v0_ref, wv2_ref, *refs):
    if first:
        out_ref, vfirst_out_ref, state_ref, zlast_ref, o_s, gate_s, mg_s = refs
    else:
        vfirst_ref, out_ref, state_ref, zlast_ref, o_s, gate_s, mg_s = refs

    @pl.when(pl.program_id(1) == 0)
    def _():
        state_ref[...] = jnp.zeros_like(state_ref)
        zlast_ref[...] = jnp.zeros_like(zlast_ref)

    h = _rmsnorm(x_ref[0], ng_ref[0]).astype(BF16)
    zc = _dot(h, wc_ref[0, :, :SHIFT_W])
    gate_s[...] = _dot(h, wc_ref[0, :, SHIFT_W:SHIFT_W + C_WIDTH])
    mg_s[...] = _dot(h, wc_ref[0, :, SHIFT_W + C_WIDTH:])

    zs = _shift_rows(zc, zlast_ref[...])
    zlast_ref[...] = zc[tc - 1:tc, :]
    zc = zc + mu_ref[0] * (zs - zc)

    r = zc[:, :C_WIDTH]
    k = zc[:, C_WIDTH:2 * C_WIDTH]
    v = zc[:, 2 * C_WIDTH:3 * C_WIDTH]
    lo = zc[:, 3 * C_WIDTH:C_SHIFT_WIDTH]
    lane = lax.broadcasted_iota(jnp.int32, lo.shape, 1)
    lo = jnp.where(lane < C_DECAY_RANK, jnp.tanh(lo), lo)
    lr = _mm(lo, wlr_ref[0])
    lw = -0.6065306597126334 * _sigmoid(w0_ref[0] + lr[:, :C_WIDTH])
    a_sig = _sigmoid(a0_ref[0] + lr[:, C_WIDTH:])

    if first:
        vfirst_out_ref[0] = v
    else:
        zv = zc[:, C_SHIFT_WIDTH:]
        v_mix = _sigmoid(v0_ref[0] + _mm(zv, wv2_ref[0]))
        v = v + (vfirst_ref[0] - v) * v_mix

    rr = lax.broadcasted_iota(jnp.int32, (C_WIDTH, C_WIDTH), 0) // C_HEAD
    cc = lax.broadcasted_iota(jnp.int32, (C_WIDTH, C_WIDTH), 1) // C_HEAD
    head_ones = jnp.where(rr == cc, 1.0, 0.0).astype(BF16)
    kk = k * kk_ref[0]
    kk = kk * lax.rsqrt(_mm(kk * kk, head_ones) + L2_EPS)
    k = k * (1.0 + (a_sig - 1.0) * ka_ref[0])

    c = WKV_CHUNK
    n_chunks = tc // c
    tri = (lax.broadcasted_iota(jnp.int32, (c, c), 0)
           >= lax.broadcasted_iota(jnp.int32, (c, c), 1))
    tri = jnp.where(tri, 1.0, 0.0).astype(BF16)
    w_split = jnp.concatenate(_split2(lw), axis=1)
    cl = []
    for i in range(n_chunks):
        part = _dot(tri, w_split[i * c:(i + 1) * c, :])
        cl.append(part[:, :C_WIDTH] + part[:, C_WIDTH:])
    cl = jnp.concatenate(cl, axis=0)
    b_all = kk * a_sig
    keys = [(i, q) for i in range(n_chunks) for q in range(N_GROUPS)]
    sl = lambda z, i, q: z[i * c:(i + 1) * c, q * GROUP:(q + 1) * GROUP]
    local = dict(zip(keys, _wkv_local(
        [tuple(sl(z, i, q) for z in (r, k, v, kk, b_all, lw, cl)) for i, q in keys])))
    hs = [state_ref[q] for q in range(N_GROUPS)]
    for i in range(n_chunks):
        for q in range(N_GROUPS):
            r_p, o_loc, m_full, n_full = local[i, q]
            both = _dot(jnp.concatenate([r_p, m_full], axis=0), hs[q].astype(BF16))
            o_s[i * c:(i + 1) * c, q * GROUP:(q + 1) * GROUP] = both[:c] + o_loc
            hs[q] = both[c:] + n_full
    for q in range(N_GROUPS):
        state_ref[q] = hs[q]

    o = o_s[...]
    mu = _mm(o, head_ones) * (1.0 / C_HEAD)
    oc = o - mu
    var = _mm(oc * oc, head_ones) * (1.0 / C_HEAD)
    o = oc * lax.rsqrt(var + GN_EPS) * gng_ref[0] + gnb_ref[0]
    o = o + _mm(r * k * rk_ref[0], head_ones) * v
    yc = o * _silu(gate_s[...])
    out_ref[0] = _sigmoid(mg_s[...]) * _mm(yc, wbc_ref[0])


def _layer_specs(layer):
    row = lambda w: pl.BlockSpec((1, 1, w), lambda b, j: (layer, 0, 0))
    full = lambda *s: pl.BlockSpec((1,) + s, lambda b, j: (layer,) + (0,) * len(s),
                                   pipeline_mode=pl.Buffered(1))
    return row, full


def _wkv_call(layer, tc, x, p, vfirst):
    first = vfirst is None
    bsz, seq, _ = x.shape
    row, full = _layer_specs(layer)
    tile = lambda w: pl.BlockSpec((1, tc, w), lambda b, j: (b, j, 0))
    in_specs = [tile(D_MODEL), row(D_MODEL), full(D_MODEL, WC_COLS), row(SHIFT_W),
                row(C_WIDTH), row(C_WIDTH), full(LANES, 2 * C_WIDTH),
                row(C_WIDTH), row(C_WIDTH), row(C_WIDTH), row(C_WIDTH), row(C_WIDTH),
                full(C_WIDTH, D_MODEL), row(C_WIDTH), full(LANES, C_WIDTH)]
    args = [x, p["norm_g"], p["wc"], p["mu"], p["w0"], p["a0"], p["wlr"], p["kk"], p["ka"], p["rk"],
            p["gn_g"], p["gn_b"], p["wbc"], p["v0"], p["wv2"]]
    out_shape = [jax.ShapeDtypeStruct((bsz, seq, D_MODEL), F32)]
    out_specs = [tile(D_MODEL)]
    if first:
        out_shape.append(jax.ShapeDtypeStruct((bsz, seq, C_WIDTH), F32))
        out_specs.append(tile(C_WIDTH))
    else:
        in_specs.append(tile(C_WIDTH))
        args.append(vfirst)
    scratch = [pltpu.VMEM((N_GROUPS, GROUP, GROUP), F32),
               pltpu.VMEM((1, SHIFT_W), F32),
               pltpu.VMEM((tc, C_WIDTH), F32), pltpu.VMEM((tc, C_WIDTH), F32),
               pltpu.VMEM((tc, D_MODEL), F32)]
    return pl.pallas_call(
        functools.partial(_wkv_kernel, first, tc),
        grid=(bsz, seq // tc),
        in_specs=in_specs,
        out_specs=out_specs,
        out_shape=out_shape,
        scratch_shapes=scratch,
        compiler_params=pltpu.CompilerParams(
            dimension_semantics=("arbitrary", "arbitrary"), vmem_limit_bytes=VMEM_LIMIT),
        name="wkv_first" if first else "wkv",
    )(*args)


def _abx_kernel(last, tm, x_ref, mc_ref, ng_ref, w_ref, alg_ref, alb_ref, aws_ref, abs_ref,
                bw_ref, bb_ref, blg_ref, blb_ref, ks_ref, vs_ref, wba_ref, wbb_ref, wbx_ref,
                wout_ref, fg_ref, out_ref, hist_ref):
    @pl.when(pl.program_id(1) == 0)
    def _():
        hist_ref[0:CONV_HIST, :] = jnp.zeros((CONV_HIST, B_WIDTH), F32)

    x = x_ref[0]
    h = _rmsnorm(x, ng_ref[0]).astype(BF16)
    col = 0

    def proj(width):
        nonlocal col
        z = _dot(h, w_ref[0, :, col:col + width])
        col += width
        return z

    u = _gelu_tanh(proj(A_WIDTH))
    vv = _layernorm(_gelu_tanh(proj(A_WIDTH)), alg_ref[0], alb_ref[0], LN_EPS)
    gate_a = _silu(proj(A_WIDTH))
    gw = A_WIDTH // A_GROUPS
    causal = (lax.broadcasted_iota(jnp.int32, (A_CHUNK, A_CHUNK), 0)
              >= lax.broadcasted_iota(jnp.int32, (A_CHUNK, A_CHUNK), 1))
    ws = [jnp.where(causal, aws_ref[0, g], 0.0).astype(BF16) for g in range(A_GROUPS)]
    vb = vv.astype(BF16)
    s_rows = []
    for n in range(tm // A_CHUNK):
        blk = vb[n * A_CHUNK:(n + 1) * A_CHUNK, :]
        s_rows.append(jnp.concatenate(
            [_dot(ws[g], blk[:, g * gw:(g + 1) * gw]) for g in range(A_GROUPS)], axis=1) + abs_ref[0])
    s = jnp.concatenate(s_rows, axis=0)
    ya = u * s * gate_a

    hb = proj(B_WIDTH)
    hb = hb * _sigmoid(proj(B_WIDTH))
    gate_b = _silu(proj(B_WIDTH))
    hist_ref[CONV_HIST:CONV_HIST + tm, :] = hb
    lead = CONV_HIST - (B_KERNEL - 1)
    acc = jnp.zeros((tm, B_WIDTH), F32) + bb_ref[0]
    for s8 in range(SUBLANES):
        n = tm if s8 == 0 else tm + SUBLANES
        part = None
        for q in range(CONV_HIST // SUBLANES + 1):
            t = SUBLANES * q + s8 - lead
            if 0 <= t < B_KERNEL:
                term = bw_ref[0, t:t + 1, :] * hist_ref[SUBLANES * q:SUBLANES * q + n, :]
                part = term if part is None else part + term
        acc = acc + part[s8:s8 + tm]
    hist_ref[0:CONV_HIST, :] = hist_ref[tm:tm + CONV_HIST, :]
    yb = _silu(_layernorm(acc, blg_ref[0], blb_ref[0], LN_EPS)) * gate_b

    qx = proj(X_WIDTH)
    sc = _mm_nt(qx, ks_ref[0, 0]) * (X_HEAD ** -0.5)
    ps = []
    for hd in range(X_HEADS):
        sh = sc[:, hd * MEM_LEN:(hd + 1) * MEM_LEN]
        e = jnp.exp(sh - jnp.max(sh, axis=-1, keepdims=True))
        ps.append(e / jnp.sum(e, axis=-1, keepdims=True))
    yx = _mm(jnp.concatenate(ps, axis=1), vs_ref[0, 0])

    merged = _sigmoid(proj(D_MODEL)) * _mm(ya, wba_ref[0])
    merged = merged + _sigmoid(proj(D_MODEL)) * _mm(yb, wbb_ref[0])
    merged = merged + _sigmoid(proj(D_MODEL)) * _mm(yx, wbx_ref[0])
    merged = merged + mc_ref[0]
    y = x + _mm(merged, wout_ref[0])
    if last:
        y = _rmsnorm(y, fg_ref[...])
    out_ref[0] = y


def _abx_call(layer, last, tm, x, mc, p, kst, vst, fg):
    bsz, seq, _ = x.shape
    row, full = _layer_specs(layer)
    tile = lambda wd: pl.BlockSpec((1, tm, wd), lambda b, j: (b, j, 0))
    mem = pl.BlockSpec((1, 1, X_HEADS * MEM_LEN, X_WIDTH), lambda b, j: (layer, b, 0, 0))
    in_specs = [tile(D_MODEL), tile(D_MODEL), row(D_MODEL), full(D_MODEL, p["wabx"].shape[2]),
                row(A_WIDTH), row(A_WIDTH), full(A_GROUPS, A_CHUNK, A_CHUNK), full(A_CHUNK, A_WIDTH),
                full(CONV_HIST, B_WIDTH), row(B_WIDTH), row(B_WIDTH), row(B_WIDTH),
                mem, mem,
                full(A_WIDTH, D_MODEL), full(B_WIDTH, D_MODEL), full(X_WIDTH, D_MODEL),
                full(D_MODEL, D_MODEL), pl.BlockSpec((1, D_MODEL), lambda b, j: (0, 0))]
    args = [x, mc, p["norm_g"], p["wabx"], p["a_ln_g"], p["a_ln_b"], p["a_ws"], p["a_bs"],
            p["b_wdw"], p["b_bdw"], p["b_ln_g"], p["b_ln_b"], kst, vst,
            p["wba"], p["wbb"], p["wbx"], p["wout"], fg]
    return pl.pallas_call(
        functools.partial(_abx_kernel, last, tm),
        grid=(bsz, seq // tm),
        in_specs=in_specs,
        out_specs=tile(D_MODEL),
        out_shape=jax.ShapeDtypeStruct((bsz, seq, D_MODEL), F32),
        scratch_shapes=[pltpu.VMEM((CONV_HIST + tm, B_WIDTH), F32)],
        compiler_params=pltpu.CompilerParams(
            dimension_semantics=("arbitrary", "arbitrary"), vmem_limit_bytes=VMEM_LIMIT),
        name="abx_last" if last else "abx",
    )(*args)


def _pick_tile(seq, want, multiple):
    t = min(want, seq)
    while seq % t or t % multiple:
        t -= multiple
    return t


def kernel(x, mem, norm_g, w_in, w_in_vres, a_ln_g, a_ln_b, a_ws, a_bs, b_wdw, b_bdw, b_ln_g, b_ln_b, c_mu, c_mu_vres, c_w0, c_ww2, c_a0, c_wa2, c_v0, c_wv2, c_kk, c_ka, c_rk, c_gn_g, c_gn_b, mem_norm_g, w_mem_kv, w_branch_a, w_branch_b, w_branch_c, w_branch_x, w_out, final_norm_g):
    depth = w_in.shape[0]
    seq = x.shape[1]
    tc = _pick_tile(seq, 512, WKV_CHUNK)
    tm = _pick_tile(seq, 512, A_CHUNK)

    o_a = 0
    o_b = o_a + 3 * A_WIDTH
    o_c = o_b + 3 * B_WIDTH
    o_cg = o_c + C_SHIFT_WIDTH
    o_x = o_cg + C_WIDTH
    o_m = o_x + X_WIDTH

    rows = lambda a: a.reshape(depth, 1, -1)
    pad_to = lambda a, n, axis: jnp.pad(a, [(0, n - a.shape[i]) if i == axis else (0, 0) for i in range(a.ndim)])
    lead0 = lambda a: jnp.concatenate([jnp.zeros_like(a[:1]), a], axis=0)
    col_cat = lambda *a: jnp.concatenate(a, axis=2)

    zeros_lr = jnp.zeros_like(c_ww2)
    wkv_p = dict(
        norm_g=rows(norm_g),
        wc=col_cat(w_in[:, :, o_c:o_cg], pad_to(lead0(w_in_vres), LANES, 2), w_in[:, :, o_cg:o_x],
                   w_in[:, :, o_m + 2 * D_MODEL:o_m + 3 * D_MODEL]).astype(BF16),
        mu=rows(jnp.concatenate([c_mu, pad_to(lead0(c_mu_vres), LANES, 1)], axis=1)),
        w0=rows(c_w0), a0=rows(c_a0),
        wlr=jnp.concatenate([col_cat(c_ww2, zeros_lr), col_cat(zeros_lr, c_wa2)], axis=1).astype(BF16),
        kk=rows(c_kk), ka=rows(c_ka), rk=rows(c_rk), gn_g=rows(c_gn_g), gn_b=rows(c_gn_b),
        wbc=w_branch_c.astype(BF16),
        v0=rows(lead0(c_v0)), wv2=pad_to(lead0(c_wv2), LANES, 1).astype(BF16))
    abx_p = dict(
        norm_g=rows(norm_g),
        wabx=col_cat(w_in[:, :, o_a:o_c], w_in[:, :, o_x:o_m + 2 * D_MODEL],
                     w_in[:, :, o_m + 3 * D_MODEL:o_m + 4 * D_MODEL]).astype(BF16),
        a_ln_g=rows(a_ln_g), a_ln_b=rows(a_ln_b), a_ws=a_ws,
        a_bs=jnp.repeat(jnp.swapaxes(a_bs, 1, 2), A_WIDTH // A_GROUPS, axis=2),
        b_wdw=pad_to(b_wdw, CONV_HIST, 1), b_bdw=rows(b_bdw), b_ln_g=rows(b_ln_g), b_ln_b=rows(b_ln_b),
        wba=w_branch_a.astype(BF16), wbb=w_branch_b.astype(BF16), wbx=w_branch_x.astype(BF16),
        wout=w_out.astype(BF16))

    kst, vst = _memkv(mem, mem_norm_g, w_mem_kv)
    fg = final_norm_g.reshape(1, D_MODEL)
    v_first = None
    for i in range(depth):
        if i == 0:
            mc, v_first = _wkv_call(i, tc, x, wkv_p, None)
        else:
            (mc,) = _wkv_call(i, tc, x, wkv_p, v_first)
        x = _abx_call(i, i == depth - 1, tm, x, mc, abx_p, kst, vst, fg)
    return x
```

```python
import functools

import jax
import jax.numpy as jnp
from jax import lax
from jax.experimental import pallas as pl
from jax.experimental.pallas import tpu as pltpu

F32 = jnp.float32
BF16 = jnp.bfloat16

D_MODEL = 1024
MEM_LEN = 256
A_WIDTH = 512
A_GROUPS = 4
A_CHUNK = 128
B_WIDTH = 512
B_KERNEL = 31
C_HEAD = 64
C_HEADS = 8
C_WIDTH = C_HEADS * C_HEAD
C_DECAY_RANK = 64
C_AAA_RANK = 64
C_VRES_RANK = 32
C_SHIFT_WIDTH = 3 * C_WIDTH + C_DECAY_RANK + C_AAA_RANK
X_HEADS = 4
X_HEAD = 64
X_WIDTH = X_HEADS * X_HEAD

RMS_EPS = 1e-6
LN_EPS = 1e-5
GN_EPS = 64e-5
L2_EPS = 1e-12

LANES = 128
SUBLANES = 8
WKV_CHUNK = 64
WKV_MID = WKV_CHUNK // 2 - 1
GROUP = 2 * C_HEAD
N_GROUPS = C_WIDTH // GROUP
SHIFT_W = C_SHIFT_WIDTH + LANES
CONV_HIST = 32
VMEM_LIMIT = 56 * 1024 * 1024


def _split2(x):
    hi = x.astype(BF16)
    lo = (x - hi.astype(F32)).astype(BF16)
    return hi, lo


def _dot(a, b):
    return jnp.dot(a, b, preferred_element_type=F32)


def _mm(a, b):
    return _dot(a.astype(BF16), b.astype(BF16))


def _mm_nt(a, b):
    return lax.dot_general(a.astype(BF16), b.astype(BF16), (((1,), (1,)), ((), ())),
                           preferred_element_type=F32)


def _rmsnorm(x, g):
    ms = jnp.mean(x * x, axis=-1, keepdims=True)
    return x * lax.rsqrt(ms + RMS_EPS) * g


def _layernorm(x, g, b, eps):
    mu = jnp.mean(x, axis=-1, keepdims=True)
    xc = x - mu
    var = jnp.mean(xc * xc, axis=-1, keepdims=True)
    return xc * lax.rsqrt(var + eps) * g + b


def _sigmoid(x):
    return jax.nn.sigmoid(x)


def _silu(x):
    return x * jax.nn.sigmoid(x)


def _gelu_tanh(x):
    c = 0.7978845608028654
    return 0.5 * x * (1.0 + jnp.tanh(c * (x + 0.044715 * (x * x * x))))


def _head_stack(x, n_heads, head_width):
    lane_head = lax.broadcasted_iota(jnp.int32, x.shape, 1) // head_width
    zero = jnp.zeros_like(x)
    return jnp.concatenate([jnp.where(lane_head == h, x, zero) for h in range(n_heads)], axis=0)


def _shift_rows(z, prev_row):
    rolled = pltpu.roll(z, 1, 0)
    row = lax.broadcasted_iota(jnp.int32, z.shape, 0)
    return jnp.where(row == 0, prev_row, rolled)


def _memkv_kernel(mem_ref, g_ref, w_ref, k_ref, v_ref):
    mem_n = _rmsnorm(mem_ref[0], g_ref[...])
    kv = _mm(mem_n, w_ref[0])
    k_ref[0, 0] = _head_stack(kv[:, :X_WIDTH], X_HEADS, X_HEAD).astype(BF16)
    v_ref[0, 0] = _head_stack(kv[:, X_WIDTH:], X_HEADS, X_HEAD).astype(BF16)


def _memkv(mem, mem_norm_g, w_mem_kv):
    bsz = mem.shape[0]
    depth = w_mem_kv.shape[0]
    out = jax.ShapeDtypeStruct((depth, bsz, X_HEADS * MEM_LEN, X_WIDTH), BF16)
    spec = pl.BlockSpec((1, 1, X_HEADS * MEM_LEN, X_WIDTH), lambda l, b: (l, b, 0, 0))
    return pl.pallas_call(
        _memkv_kernel,
        grid=(depth, bsz),
        in_specs=[pl.BlockSpec((1, MEM_LEN, D_MODEL), lambda l, b: (b, 0, 0)),
                  pl.BlockSpec((1, D_MODEL), lambda l, b: (0, 0)),
                  pl.BlockSpec((1, D_MODEL, 2 * X_WIDTH), lambda l, b: (l, 0, 0))],
        out_specs=[spec, spec],
        out_shape=[out, out],
        name="memkv",
    )(mem, mem_norm_g.reshape(1, D_MODEL), w_mem_kv)


def _wkv_local(chains):
    c = WKV_CHUNK
    gw = GROUP
    nh = gw // C_HEAD
    stack = lambda z: _head_stack(z.astype(BF16), nh, C_HEAD)
    cat0 = lambda *zs: jnp.concatenate(zs, axis=0)
    cat1 = lambda *zs: jnp.concatenate(zs, axis=1)
    row = lax.broadcasted_iota(jnp.int32, (c, gw), 0)
    col = lax.broadcasted_iota(jnp.int32, (c, gw), 1) % c
    strict = row > col
    incl = row >= col
    eye = jnp.where(row == col, 1.0, 0.0)
    zero = jnp.zeros((c, gw), F32)
    rowk = lax.broadcasted_iota(jnp.int32, (gw, gw), 0)
    colk = lax.broadcasted_iota(jnp.int32, (gw, gw), 1)
    same_head = (rowk // C_HEAD) == (colk // C_HEAD)
    diag = rowk == colk
    zq = jnp.zeros((gw, gw), F32)

    prep = []
    for r, k, v, kk, b, lw, cl in chains:
        m = cl[WKV_MID:WKV_MID + 1, :]
        e_fwd = jnp.exp(cl - m)
        e_inv = jnp.exp(m - cl)
        e_mid = jnp.exp(m)
        e_end = jnp.exp(cl[c - 1:c, :] - m)
        r_t = r * e_fwd
        a_t = -kk * jnp.exp(cl - lw - m)
        b_t = b * e_inv
        k_t = k * e_inv
        prep.append(dict(
            v=v, r_t=r_t, a_t=a_t, b_t=b_t, k_t=k_t,
            r_bar=r_t * e_mid,
            a_bar=a_t * e_mid,
            bk_hat=cat1((b_t * e_end).T, (k_t * e_end).T).astype(BF16),
            d_end=e_end * e_mid))

    g = [_mm_nt(cat0(p["a_t"], p["r_t"]), cat0(stack(p["b_t"]), stack(p["k_t"]))) for p in prep]
    l_ab = [jnp.where(strict, x[:c, :gw], zero) for x in g]
    l_ak = [jnp.where(strict, x[:c, gw:], zero) for x in g]
    a_rb = [jnp.where(incl, x[c:, :gw], zero) for x in g]
    a_rk = [jnp.where(incl, x[c:, gw:], zero) for x in g]

    p_acc = [eye + x for x in l_ab]
    q_pow = [_mm(x, stack(x)) for x in l_ab]
    for _ in range(4):
        step = [_mm(cat0(p, q), stack(q)) for p, q in zip(p_acc, q_pow)]
        p_acc = [p + x[:c] for p, x in zip(p_acc, step)]
        q_pow = [x[c:] for x in step]
    t_inv = [p + _mm(p, stack(q)) for p, q in zip(p_acc, q_pow)]

    lv = [_mm(cat0(x, y), stack(p["v"])) for x, y, p in zip(l_ak, a_rk, prep)]
    tva = [_mm(t, cat1(stack(x[:c]), stack(p["a_bar"]))) for t, x, p in zip(t_inv, lv, prep)]
    tv = [x[:, :gw] for x in tva]
    a_p = [x[:, gw:] for x in tva]
    ro = [_mm(x, cat1(stack(y), stack(z))) for x, y, z in zip(a_rb, a_p, tv)]
    r_p = [(p["r_bar"] + x[:, :gw]).astype(BF16) for p, x in zip(prep, ro)]
    o_loc = [x[:, gw:] + y[c:] for x, y in zip(ro, lv)]
    m_full = [(jnp.where(diag, jnp.broadcast_to(p["d_end"], (gw, gw)), zq)
               + jnp.where(same_head, _mm(p["bk_hat"][:, :c], x), zq)).astype(BF16)
              for p, x in zip(prep, a_p)]
    n_full = [jnp.where(same_head, _mm(p["bk_hat"], cat0(x, p["v"])), zq) for p, x in zip(prep, tv)]
    return list(zip(r_p, o_loc, m_full, n_full))


def _wkv_kernel(first, tc, x_ref, ng_ref, wzc_ref, wvr_ref, wgate_ref, wmg_ref, mu_ref, w0_ref, a0_ref,
                wlr_ref, kk_ref, ka_ref, rk_ref, gng_ref, gnb_ref, wbc_ref, v0_ref, wv2_ref, *refs):
    if first:
        out_ref, vfirst_out_ref, state_ref, zlast_ref, o_s, gate_s, mg_s = refs
    else:
        vfirst_ref, out_ref, state_ref, zlast_ref, o_s, gate_s, mg_s = refs

    @pl.when(pl.program_id(1) == 0)
    def _():
        state_ref[...] = jnp.zeros_like(state_ref)
        zlast_ref[...] = jnp.zeros_like(zlast_ref)

    h = _rmsnorm(x_ref[0], ng_ref[0]).astype(BF16)
    zc = jnp.concatenate([_dot(h, wzc_ref[0]), _dot(h, wvr_ref[0])], axis=1)
    gate_s[...] = _dot(h, wgate_ref[0])
    mg_s[...] = _dot(h, wmg_ref[0])

    zs = _shift_rows(zc, zlast_ref[...])
    zlast_ref[...] = zc[tc - 1:tc, :]
    zc = zc + mu_ref[0] * (zs - zc)

    r = zc[:, :C_WIDTH]
    k = zc[:, C_WIDTH:2 * C_WIDTH]
    v = zc[:, 2 * C_WIDTH:3 * C_WIDTH]
    lo = zc[:, 3 * C_WIDTH:C_SHIFT_WIDTH]
    lane = lax.broadcasted_iota(jnp.int32, lo.shape, 1)
    lo = jnp.where(lane < C_DECAY_RANK, jnp.tanh(lo), lo)
    lr = _mm(lo, wlr_ref[0])
    lw = -0.6065306597126334 * _sigmoid(w0_ref[0] + lr[:, :C_WIDTH])
    a_sig = _sigmoid(a0_ref[0] + lr[:, C_WIDTH:])

    if first:
        vfirst_out_ref[0] = v
    else:
        zv = zc[:, C_SHIFT_WIDTH:]
        v_mix = _sigmoid(v0_ref[0] + _mm(zv, wv2_ref[0]))
        v = v + (vfirst_ref[0] - v) * v_mix

    sum_w = 2 * LANES
    rr = lax.broadcasted_iota(jnp.int32, (sum_w, sum_w), 0) // C_HEAD
    cc = lax.broadcasted_iota(jnp.int32, (sum_w, sum_w), 1) // C_HEAD
    head_ones = jnp.where(rr == cc, 1.0, 0.0).astype(BF16)

    def head_sum(z):
        return jnp.concatenate([_mm(z[:, i:i + sum_w], head_ones) for i in range(0, C_WIDTH, sum_w)], axis=1)

    kk = k * kk_ref[0]
    kk = kk * lax.rsqrt(head_sum(kk * kk) + L2_EPS)
    k = k * (1.0 + (a_sig - 1.0) * ka_ref[0])

    c = WKV_CHUNK
    n_chunks = tc // c
    tri = (lax.broadcasted_iota(jnp.int32, (c, c), 0)
           >= lax.broadcasted_iota(jnp.int32, (c, c), 1))
    tri = jnp.where(tri, 1.0, 0.0).astype(BF16)
    w_split = jnp.concatenate(_split2(lw), axis=1)
    cl = []
    for i in range(n_chunks):
        part = _dot(tri, w_split[i * c:(i + 1) * c, :])
        cl.append(part[:, :C_WIDTH] + part[:, C_WIDTH:])
    cl = jnp.concatenate(cl, axis=0)
    b_all = kk * a_sig
    keys = [(i, q) for i in range(n_chunks) for q in range(N_GROUPS)]
    sl = lambda z, i, q: z[i * c:(i + 1) * c, q * GROUP:(q + 1) * GROUP]
    local = dict(zip(keys, _wkv_local(
        [tuple(sl(z, i, q) for z in (r, k, v, kk, b_all, lw, cl)) for i, q in keys])))
    hs = [state_ref[q] for q in range(N_GROUPS)]
    for i in range(n_chunks):
        for q in range(N_GROUPS):
            r_p, o_loc, m_full, n_full = local[i, q]
            both = _dot(jnp.concatenate([r_p, m_full], axis=0), hs[q].astype(BF16))
            o_s[i * c:(i + 1) * c, q * GROUP:(q + 1) * GROUP] = both[:c] + o_loc
            hs[q] = both[c:] + n_full
    for q in range(N_GROUPS):
        state_ref[q] = hs[q]

    o = o_s[...]
    mu = head_sum(o) * (1.0 / C_HEAD)
    oc = o - mu
    var = head_sum(oc * oc) * (1.0 / C_HEAD)
    o = oc * lax.rsqrt(var + GN_EPS) * gng_ref[0] + gnb_ref[0]
    o = o + head_sum(r * k * rk_ref[0]) * v
    yc = o * _silu(gate_s[...])
    out_ref[0] = _sigmoid(mg_s[...]) * _mm(yc, wbc_ref[0])


def _layer_specs(layer):
    row = lambda w: pl.BlockSpec((1, 1, w), lambda b, j: (layer, 0, 0))
    full = lambda *s: pl.BlockSpec((1,) + s, lambda b, j: (layer,) + (0,) * len(s),
                                   pipeline_mode=pl.Buffered(1))
    win = lambda width, start: pl.BlockSpec((pl.Element(1), pl.Element(D_MODEL), pl.Element(width)),
                                            lambda b, j: (layer, 0, start), pipeline_mode=pl.Buffered(1))
    return row, full, win


def _wkv_call(layer, tc, x, p, vfirst):
    first = vfirst is None
    bsz, seq, _ = x.shape
    row, full, win = _layer_specs(layer)
    tile = lambda w: pl.BlockSpec((1, tc, w), lambda b, j: (b, j, 0))
    in_specs = [tile(D_MODEL), row(D_MODEL), win(C_SHIFT_WIDTH, p["o_c"]), full(D_MODEL, LANES),
                win(C_WIDTH, p["o_cg"]), win(D_MODEL, p["o_mc"]), row(SHIFT_W),
                row(C_WIDTH), row(C_WIDTH), full(LANES, 2 * C_WIDTH),
                row(C_WIDTH), row(C_WIDTH), row(C_WIDTH), row(C_WIDTH), row(C_WIDTH),
                full(C_WIDTH, D_MODEL), row(C_WIDTH), full(LANES, C_WIDTH)]
    args = [x, p["norm_g"], p["w_in"], p["w_vres"], p["w_in"], p["w_in"], p["mu"], p["w0"], p["a0"], p["wlr"],
            p["kk"], p["ka"], p["rk"], p["gn_g"], p["gn_b"], p["wbc"], p["v0"], p["wv2"]]
    out_shape = [jax.ShapeDtypeStruct((bsz, seq, D_MODEL), F32)]
    out_specs = [tile(D_MODEL)]
    if first:
        out_shape.append(jax.ShapeDtypeStruct((bsz, seq, C_WIDTH), F32))
        out_specs.append(tile(C_WIDTH))
    else:
        in_specs.append(tile(C_WIDTH))
        args.append(vfirst)
    scratch = [pltpu.VMEM((N_GROUPS, GROUP, GROUP), F32),
               pltpu.VMEM((1, SHIFT_W), F32),
               pltpu.VMEM((tc, C_WIDTH), F32), pltpu.VMEM((tc, C_WIDTH), F32),
               pltpu.VMEM((tc, D_MODEL), F32)]
    return pl.pallas_call(
        functools.partial(_wkv_kernel, first, tc),
        grid=(bsz, seq // tc),
        in_specs=in_specs,
        out_specs=out_specs,
        out_shape=out_shape,
        scratch_shapes=scratch,
        compiler_params=pltpu.CompilerParams(
            dimension_semantics=("arbitrary", "arbitrary"), vmem_limit_bytes=VMEM_LIMIT),
        name="wkv_first" if first else "wkv",
    )(*args)


def _abx_kernel(last, tm, x_ref, mc_ref, ng_ref, wab_ref, wqm_ref, wmx_ref, alg_ref, alb_ref, aws_ref, abs_ref,
                bw_ref, bb_ref, blg_ref, blb_ref, ks_ref, vs_ref, wba_ref, wbb_ref, wbx_ref,
                wout_ref, fg_ref, out_ref, hist_ref):
    @pl.when(pl.program_id(1) == 0)
    def _():
        hist_ref[0:CONV_HIST, :] = jnp.zeros((CONV_HIST, B_WIDTH), F32)

    x = x_ref[0]
    h = _rmsnorm(x, ng_ref[0]).astype(BF16)
    cols = {id(wab_ref): 0, id(wqm_ref): 0, id(wmx_ref): 0}

    def proj(w_ref, width):
        start = cols[id(w_ref)]
        cols[id(w_ref)] = start + width
        return _dot(h, w_ref[0, :, start:start + width])

    u = _gelu_tanh(proj(wab_ref, A_WIDTH))
    vv = _layernorm(_gelu_tanh(proj(wab_ref, A_WIDTH)), alg_ref[0], alb_ref[0], LN_EPS)
    gate_a = _silu(proj(wab_ref, A_WIDTH))
    gw = A_WIDTH // A_GROUPS
    causal = (lax.broadcasted_iota(jnp.int32, (A_CHUNK, A_CHUNK), 0)
              >= lax.broadcasted_iota(jnp.int32, (A_CHUNK, A_CHUNK), 1))
    ws = [jnp.where(causal, aws_ref[0, g], 0.0).astype(BF16) for g in range(A_GROUPS)]
    vb = vv.astype(BF16)
    s_rows = []
    for n in range(tm // A_CHUNK):
        blk = vb[n * A_CHUNK:(n + 1) * A_CHUNK, :]
        s_rows.append(jnp.concatenate(
            [_dot(ws[g], blk[:, g * gw:(g + 1) * gw]) for g in range(A_GROUPS)], axis=1) + abs_ref[0])
    s = jnp.concatenate(s_rows, axis=0)
    ya = u * s * gate_a

    hb = proj(wab_ref, B_WIDTH)
    hb = hb * _sigmoid(proj(wab_ref, B_WIDTH))
    gate_b = _silu(proj(wab_ref, B_WIDTH))
    hist_ref[CONV_HIST:CONV_HIST + tm, :] = hb
    lead = CONV_HIST - (B_KERNEL - 1)
    acc = jnp.zeros((tm, B_WIDTH), F32) + bb_ref[0]
    for s8 in range(SUBLANES):
        n = tm if s8 == 0 else tm + SUBLANES
        part = None
        for q in range(CONV_HIST // SUBLANES + 1):
            t = SUBLANES * q + s8 - lead
            if 0 <= t < B_KERNEL:
                term = bw_ref[0, t:t + 1, :] * hist_ref[SUBLANES * q:SUBLANES * q + n, :]
                part = term if part is None else part + term
        acc = acc + part[s8:s8 + tm]
    hist_ref[0:CONV_HIST, :] = hist_ref[tm:tm + CONV_HIST, :]
    yb = _silu(_layernorm(acc, blg_ref[0], blb_ref[0], LN_EPS)) * gate_b

    qx = proj(wqm_ref, X_WIDTH)
    sc = _mm_nt(qx, ks_ref[0, 0]) * (X_HEAD ** -0.5)
    ps = []
    for hd in range(X_HEADS):
        sh = sc[:, hd * MEM_LEN:(hd + 1) * MEM_LEN]
        e = jnp.exp(sh - jnp.max(sh, axis=-1, keepdims=True))
        ps.append(e * (1.0 / jnp.sum(e, axis=-1, keepdims=True)))
    yx = _mm(jnp.concatenate(ps, axis=1), vs_ref[0, 0])

    merged = _sigmoid(proj(wqm_ref, D_MODEL)) * _mm(ya, wba_ref[0])
    merged = merged + _sigmoid(proj(wqm_ref, D_MODEL)) * _mm(yb, wbb_ref[0])
    merged = merged + _sigmoid(proj(wmx_ref, D_MODEL)) * _mm(yx, wbx_ref[0])
    merged = merged + mc_ref[0]
    y = x + _mm(merged, wout_ref[0])
    if last:
        y = _rmsnorm(y, fg_ref[...])
    out_ref[0] = y


def _abx_call(layer, last, tm, x, mc, p, kst, vst, fg):
    bsz, seq, _ = x.shape
    row, full, win = _layer_specs(layer)
    tile = lambda wd: pl.BlockSpec((1, tm, wd), lambda b, j: (b, j, 0))
    mem = pl.BlockSpec((1, 1, X_HEADS * MEM_LEN, X_WIDTH), lambda b, j: (layer, b, 0, 0))
    in_specs = [tile(D_MODEL), tile(D_MODEL), row(D_MODEL),
                win(3 * A_WIDTH + 3 * B_WIDTH, p["o_a"]), win(X_WIDTH + 2 * D_MODEL, p["o_x"]),
                win(D_MODEL, p["o_mx"]),
                row(A_WIDTH), row(A_WIDTH), full(A_GROUPS, A_CHUNK, A_CHUNK), full(A_CHUNK, A_WIDTH),
                full(CONV_HIST, B_WIDTH), row(B_WIDTH), row(B_WIDTH), row(B_WIDTH),
                mem, mem,
                full(A_WIDTH, D_MODEL), full(B_WIDTH, D_MODEL), full(X_WIDTH, D_MODEL),
                full(D_MODEL, D_MODEL), pl.BlockSpec((1, D_MODEL), lambda b, j: (0, 0))]
    args = [x, mc, p["norm_g"], p["w_in"], p["w_in"], p["w_in"], p["a_ln_g"], p["a_ln_b"], p["a_ws"], p["a_bs"],
            p["b_wdw"], p["b_bdw"], p["b_ln_g"], p["b_ln_b"], kst, vst,
            p["wba"], p["wbb"], p["wbx"], p["wout"], fg]
    return pl.pallas_call(
        functools.partial(_abx_kernel, last, tm),
        grid=(bsz, seq // tm),
        in_specs=in_specs,
        out_specs=tile(D_MODEL),
        out_shape=jax.ShapeDtypeStruct((bsz, seq, D_MODEL), F32),
        scratch_shapes=[pltpu.VMEM((CONV_HIST + tm, B_WIDTH), F32)],
        compiler_params=pltpu.CompilerParams(
            dimension_semantics=("arbitrary", "arbitrary"), vmem_limit_bytes=VMEM_LIMIT),
        name="abx_last" if last else "abx",
    )(*args)


def _pick_tile(seq, want, multiple):
    t = min(want, seq)
    while seq % t or t % multiple:
        t -= multiple
    return t


def kernel(x, mem, norm_g, w_in, w_in_vres, a_ln_g, a_ln_b, a_ws, a_bs, b_wdw, b_bdw, b_ln_g, b_ln_b, c_mu, c_mu_vres, c_w0, c_ww2, c_a0, c_wa2, c_v0, c_wv2, c_kk, c_ka, c_rk, c_gn_g, c_gn_b, mem_norm_g, w_mem_kv, w_branch_a, w_branch_b, w_branch_c, w_branch_x, w_out, final_norm_g):
    depth = w_in.shape[0]
    seq = x.shape[1]
    tc = _pick_tile(seq, 512, WKV_CHUNK)
    tm = _pick_tile(seq, 512, A_CHUNK)

    o_a = 0
    o_b = o_a + 3 * A_WIDTH
    o_c = o_b + 3 * B_WIDTH
    o_cg = o_c + C_SHIFT_WIDTH
    o_x = o_cg + C_WIDTH
    o_m = o_x + X_WIDTH

    rows = lambda a: a.reshape(depth, 1, -1)
    pad_to = lambda a, n, axis: jnp.pad(a, [(0, n - a.shape[i]) if i == axis else (0, 0) for i in range(a.ndim)])
    lead0 = lambda a: jnp.concatenate([jnp.zeros_like(a[:1]), a], axis=0)
    col_cat = lambda *a: jnp.concatenate(a, axis=2)

    w_in_bf = w_in.astype(BF16)
    zeros_lr = jnp.zeros_like(c_ww2)
    wkv_p = dict(
        norm_g=rows(norm_g), w_in=w_in_bf, o_c=o_c, o_cg=o_cg, o_mc=o_m + 2 * D_MODEL,
        w_vres=pad_to(lead0(w_in_vres), LANES, 2).astype(BF16),
        mu=rows(jnp.concatenate([c_mu, pad_to(lead0(c_mu_vres), LANES, 1)], axis=1)),
        w0=rows(c_w0), a0=rows(c_a0),
        wlr=jnp.concatenate([col_cat(c_ww2, zeros_lr), col_cat(zeros_lr, c_wa2)], axis=1).astype(BF16),
        kk=rows(c_kk), ka=rows(c_ka), rk=rows(c_rk), gn_g=rows(c_gn_g), gn_b=rows(c_gn_b),
        wbc=w_branch_c.astype(BF16),
        v0=rows(lead0(c_v0)), wv2=pad_to(lead0(c_wv2), LANES, 1).astype(BF16))
    abx_p = dict(
        norm_g=rows(norm_g), w_in=w_in_bf, o_a=o_a, o_x=o_x, o_mx=o_m + 3 * D_MODEL,
        a_ln_g=rows(a_ln_g), a_ln_b=rows(a_ln_b), a_ws=a_ws,
        a_bs=jnp.repeat(jnp.swapaxes(a_bs, 1, 2), A_WIDTH // A_GROUPS, axis=2),
        b_wdw=pad_to(b_wdw, CONV_HIST, 1), b_bdw=rows(b_bdw), b_ln_g=rows(b_ln_g), b_ln_b=rows(b_ln_b),
        wba=w_branch_a.astype(BF16), wbb=w_branch_b.astype(BF16), wbx=w_branch_x.astype(BF16),
        wout=w_out.astype(BF16))

    kst, vst = _memkv(mem, mem_norm_g, w_mem_kv)
    fg = final_norm_g.reshape(1, D_MODEL)
    v_first = None
    for i in range(depth):
        if i == 0:
            mc, v_first = _wkv_call(i, tc, x, wkv_p, None)
        else:
            (mc,) = _wkv_call(i, tc, x, wkv_p, v_first)
        x = _abx_call(i, i == depth - 1, tm, x, mc, abx_p, kst, vst, fg)
    return x
```

```python
import functools

import jax
import jax.numpy as jnp
from jax import lax
from jax.experimental import pallas as pl
from jax.experimental.pallas import tpu as pltpu

F32 = jnp.float32
BF16 = jnp.bfloat16

D_MODEL = 1024
MEM_LEN = 256
A_WIDTH = 512
A_GROUPS = 4
A_CHUNK = 128
B_WIDTH = 512
B_KERNEL = 31
C_HEAD = 64
C_HEADS = 8
C_WIDTH = C_HEADS * C_HEAD
C_DECAY_RANK = 64
C_AAA_RANK = 64
C_VRES_RANK = 32
C_SHIFT_WIDTH = 3 * C_WIDTH + C_DECAY_RANK + C_AAA_RANK
X_HEADS = 4
X_HEAD = 64
X_WIDTH = X_HEADS * X_HEAD

RMS_EPS = 1e-6
LN_EPS = 1e-5
GN_EPS = 64e-5
L2_EPS = 1e-12

LANES = 128
SUBLANES = 8
WKV_CHUNK = 64
WKV_MID = WKV_CHUNK // 2 - 1
GROUP = 2 * C_HEAD
N_GROUPS = C_WIDTH // GROUP
SHIFT_W = C_SHIFT_WIDTH + LANES
CONV_HIST = 32
VMEM_LIMIT = 56 * 1024 * 1024


def _split2(x):
    hi = x.astype(BF16)
    lo = (x - hi.astype(F32)).astype(BF16)
    return hi, lo


def _dot(a, b):
    return jnp.dot(a, b, preferred_element_type=F32)


def _mm(a, b):
    return _dot(a.astype(BF16), b.astype(BF16))


def _mm_nt(a, b):
    return lax.dot_general(a.astype(BF16), b.astype(BF16), (((1,), (1,)), ((), ())),
                           preferred_element_type=F32)


def _rmsnorm(x, g):
    ms = jnp.mean(x * x, axis=-1, keepdims=True)
    return x * lax.rsqrt(ms + RMS_EPS) * g


def _layernorm(x, g, b, eps):
    mu = jnp.mean(x, axis=-1, keepdims=True)
    xc = x - mu
    var = jnp.mean(xc * xc, axis=-1, keepdims=True)
    return xc * lax.rsqrt(var + eps) * g + b


def _sigmoid(x):
    return jax.nn.sigmoid(x)


def _silu(x):
    return x * jax.nn.sigmoid(x)


def _gelu_tanh(x):
    c = 0.7978845608028654
    return 0.5 * x * (1.0 + jnp.tanh(c * (x + 0.044715 * (x * x * x))))


def _head_stack(x, n_heads, head_width):
    lane_head = lax.broadcasted_iota(jnp.int32, x.shape, 1) // head_width
    zero = jnp.zeros_like(x)
    return jnp.concatenate([jnp.where(lane_head == h, x, zero) for h in range(n_heads)], axis=0)


def _shift_rows(z, prev_row):
    rolled = pltpu.roll(z, 1, 0)
    row = lax.broadcasted_iota(jnp.int32, z.shape, 0)
    return jnp.where(row == 0, prev_row, rolled)


def _memkv_kernel(mem_ref, g_ref, w_ref, k_ref, v_ref):
    mem_n = _rmsnorm(mem_ref[0], g_ref[...])
    kv = _mm(mem_n, w_ref[0])
    k_ref[0, 0] = _head_stack(kv[:, :X_WIDTH], X_HEADS, X_HEAD).astype(BF16)
    v_ref[0, 0] = _head_stack(kv[:, X_WIDTH:], X_HEADS, X_HEAD).astype(BF16)


def _memkv(mem, mem_norm_g, w_mem_kv):
    bsz = mem.shape[0]
    depth = w_mem_kv.shape[0]
    out = jax.ShapeDtypeStruct((depth, bsz, X_HEADS * MEM_LEN, X_WIDTH), BF16)
    spec = pl.BlockSpec((1, 1, X_HEADS * MEM_LEN, X_WIDTH), lambda l, b: (l, b, 0, 0))
    return pl.pallas_call(
        _memkv_kernel,
        grid=(depth, bsz),
        in_specs=[pl.BlockSpec((1, MEM_LEN, D_MODEL), lambda l, b: (b, 0, 0)),
                  pl.BlockSpec((1, D_MODEL), lambda l, b: (0, 0)),
                  pl.BlockSpec((1, D_MODEL, 2 * X_WIDTH), lambda l, b: (l, 0, 0))],
        out_specs=[spec, spec],
        out_shape=[out, out],
        name="memkv",
    )(mem, mem_norm_g.reshape(1, D_MODEL), w_mem_kv)


def _wkv_local(chains, between=lambda: None):
    c = WKV_CHUNK
    gw = GROUP
    nh = gw // C_HEAD
    stack = lambda z: _head_stack(z.astype(BF16), nh, C_HEAD)
    cat0 = lambda *zs: jnp.concatenate(zs, axis=0)
    cat1 = lambda *zs: jnp.concatenate(zs, axis=1)
    row = lax.broadcasted_iota(jnp.int32, (c, gw), 0)
    col = lax.broadcasted_iota(jnp.int32, (c, gw), 1) % c
    strict = row > col
    incl = row >= col
    eye = jnp.where(row == col, 1.0, 0.0)
    zero = jnp.zeros((c, gw), F32)
    rowk = lax.broadcasted_iota(jnp.int32, (gw, gw), 0)
    colk = lax.broadcasted_iota(jnp.int32, (gw, gw), 1)
    same_head = (rowk // C_HEAD) == (colk // C_HEAD)
    diag = rowk == colk
    zq = jnp.zeros((gw, gw), F32)

    prep = []
    for r, k, v, kk, b, lw, cl in chains:
        m = cl[WKV_MID:WKV_MID + 1, :]
        e_fwd = jnp.exp(cl - m)
        e_inv = jnp.exp(m - cl)
        e_mid = jnp.exp(m)
        e_end = jnp.exp(cl[c - 1:c, :] - m)
        r_t = r * e_fwd
        a_t = -kk * jnp.exp(cl - lw - m)
        b_t = b * e_inv
        k_t = k * e_inv
        prep.append(dict(
            v=v, r_t=r_t, a_t=a_t, b_t=b_t, k_t=k_t,
            r_bar=r_t * e_mid,
            a_bar=a_t * e_mid,
            bk_hat=cat1((b_t * e_end).T, (k_t * e_end).T).astype(BF16),
            d_end=e_end * e_mid))

    g = [_mm_nt(cat0(p["a_t"], p["r_t"]), cat0(stack(p["b_t"]), stack(p["k_t"]))) for p in prep]
    between()
    l_ab = [jnp.where(strict, x[:c, :gw], zero) for x in g]
    l_ak = [jnp.where(strict, x[:c, gw:], zero) for x in g]
    a_rb = [jnp.where(incl, x[c:, :gw], zero) for x in g]
    a_rk = [jnp.where(incl, x[c:, gw:], zero) for x in g]

    p_acc = [eye + x for x in l_ab]
    q_pow = [_mm(x, stack(x)) for x in l_ab]
    between()
    for _ in range(4):
        step = [_mm(cat0(p, q), stack(q)) for p, q in zip(p_acc, q_pow)]
        between()
        p_acc = [p + x[:c] for p, x in zip(p_acc, step)]
        q_pow = [x[c:] for x in step]
    t_inv = [p + _mm(p, stack(q)) for p, q in zip(p_acc, q_pow)]
    between()

    lv = [_mm(cat0(x, y), stack(p["v"])) for x, y, p in zip(l_ak, a_rk, prep)]
    tva = [_mm(t, cat1(stack(x[:c]), stack(p["a_bar"]))) for t, x, p in zip(t_inv, lv, prep)]
    tv = [x[:, :gw] for x in tva]
    a_p = [x[:, gw:] for x in tva]
    ro = [_mm(x, cat1(stack(y), stack(z))) for x, y, z in zip(a_rb, a_p, tv)]
    r_p = [(p["r_bar"] + x[:, :gw]).astype(BF16) for p, x in zip(prep, ro)]
    o_loc = [x[:, gw:] + y[c:] for x, y in zip(ro, lv)]
    m_full = [(jnp.where(diag, jnp.broadcast_to(p["d_end"], (gw, gw)), zq)
               + jnp.where(same_head, _mm(p["bk_hat"][:, :c], x), zq)).astype(BF16)
              for p, x in zip(prep, a_p)]
    n_full = [jnp.where(same_head, _mm(p["bk_hat"], cat0(x, p["v"])), zq) for p, x in zip(prep, tv)]
    return list(zip(r_p, o_loc, m_full, n_full))


def _wkv_kernel(first, tc, x_ref, ng_ref, wzc_ref, wvr_ref, wgate_ref, wmg_ref, mu_ref, w0_ref, a0_ref,
                wlr_ref, kk_ref, ka_ref, rk_ref, gng_ref, gnb_ref, wbc_ref, v0_ref, wv2_ref, *refs):
    if first:
        out_ref, vfirst_out_ref, state_ref, zlast_ref, o_s, gate_s, mg_s = refs
    else:
        vfirst_ref, out_ref, state_ref, zlast_ref, o_s, gate_s, mg_s = refs

    @pl.when(pl.program_id(1) == 0)
    def _():
        state_ref[...] = jnp.zeros_like(state_ref)
        zlast_ref[...] = jnp.zeros_like(zlast_ref)

    h = _rmsnorm(x_ref[0], ng_ref[0]).astype(BF16)
    zc = jnp.concatenate([_dot(h, wzc_ref[0]), _dot(h, wvr_ref[0])], axis=1)
    gate_s[...] = _dot(h, wgate_ref[0])
    mg_s[...] = _dot(h, wmg_ref[0])

    zs = _shift_rows(zc, zlast_ref[...])
    zlast_ref[...] = zc[tc - 1:tc, :]
    zc = zc + mu_ref[0] * (zs - zc)

    r = zc[:, :C_WIDTH]
    k = zc[:, C_WIDTH:2 * C_WIDTH]
    v = zc[:, 2 * C_WIDTH:3 * C_WIDTH]
    lo = zc[:, 3 * C_WIDTH:C_SHIFT_WIDTH]
    lane = lax.broadcasted_iota(jnp.int32, lo.shape, 1)
    lo = jnp.where(lane < C_DECAY_RANK, jnp.tanh(lo), lo)
    lr = _mm(lo, wlr_ref[0])
    lw = -0.6065306597126334 * _sigmoid(w0_ref[0] + lr[:, :C_WIDTH])
    a_sig = _sigmoid(a0_ref[0] + lr[:, C_WIDTH:])

    if first:
        vfirst_out_ref[0] = v
    else:
        zv = zc[:, C_SHIFT_WIDTH:]
        v_mix = _sigmoid(v0_ref[0] + _mm(zv, wv2_ref[0]))
        v = v + (vfirst_ref[0] - v) * v_mix

    sum_w = 2 * LANES
    rr = lax.broadcasted_iota(jnp.int32, (sum_w, sum_w), 0) // C_HEAD
    cc = lax.broadcasted_iota(jnp.int32, (sum_w, sum_w), 1) // C_HEAD
    head_ones = jnp.where(rr == cc, 1.0, 0.0).astype(BF16)

    def head_sum(z):
        return jnp.concatenate([_mm(z[:, i:i + sum_w], head_ones) for i in range(0, C_WIDTH, sum_w)], axis=1)

    kk = k * kk_ref[0]
    kk = kk * lax.rsqrt(head_sum(kk * kk) + L2_EPS)
    k = k * (1.0 + (a_sig - 1.0) * ka_ref[0])

    c = WKV_CHUNK
    n_chunks = tc // c
    tri = (lax.broadcasted_iota(jnp.int32, (c, c), 0)
           >= lax.broadcasted_iota(jnp.int32, (c, c), 1))
    tri = jnp.where(tri, 1.0, 0.0).astype(BF16)
    w_split = jnp.concatenate(_split2(lw), axis=1)
    cl = []
    for i in range(n_chunks):
        part = _dot(tri, w_split[i * c:(i + 1) * c, :])
        cl.append(part[:, :C_WIDTH] + part[:, C_WIDTH:])
    cl = jnp.concatenate(cl, axis=0)
    b_all = kk * a_sig
    sl = lambda z, i, q: z[i * c:(i + 1) * c, q * GROUP:(q + 1) * GROUP]
    hs = [state_ref[q] for q in range(N_GROUPS)]
    pending = []

    def carry_step():
        if pending:
            i, q, (r_p, o_loc, m_full, n_full) = pending.pop(0)
            both = _dot(jnp.concatenate([r_p, m_full], axis=0), hs[q].astype(BF16))
            o_s[i * c:(i + 1) * c, q * GROUP:(q + 1) * GROUP] = both[:c] + o_loc
            hs[q] = both[c:] + n_full

    def carry_chunk():
        for _ in range(N_GROUPS):
            carry_step()

    wave = max(1, n_chunks // 2)
    for w0 in range(0, n_chunks, wave):
        keys = [(i, q) for i in range(w0, min(w0 + wave, n_chunks)) for q in range(N_GROUPS)]
        local = _wkv_local([tuple(sl(z, i, q) for z in (r, k, v, kk, b_all, lw, cl)) for i, q in keys],
                           between=carry_chunk)
        pending.extend((i, q, m) for (i, q), m in zip(keys, local))
    while pending:
        carry_step()
    for q in range(N_GROUPS):
        state_ref[q] = hs[q]

    o = o_s[...]
    mu = head_sum(o) * (1.0 / C_HEAD)
    oc = o - mu
    var = head_sum(oc * oc) * (1.0 / C_HEAD)
    o = oc * lax.rsqrt(var + GN_EPS) * gng_ref[0] + gnb_ref[0]
    o = o + head_sum(r * k * rk_ref[0]) * v
    yc = o * _silu(gate_s[...])
    out_ref[0] = _sigmoid(mg_s[...]) * _mm(yc, wbc_ref[0])


def _layer_specs(layer):
    row = lambda w: pl.BlockSpec((1, 1, w), lambda b, j: (layer, 0, 0))
    full = lambda *s: pl.BlockSpec((1,) + s, lambda b, j: (layer,) + (0,) * len(s),
                                   pipeline_mode=pl.Buffered(1))
    win = lambda width, start: pl.BlockSpec((pl.Element(1), pl.Element(D_MODEL), pl.Element(width)),
                                            lambda b, j: (layer, 0, start), pipeline_mode=pl.Buffered(1))
    return row, full, win


def _wkv_call(layer, tc, x, p, vfirst):
    first = vfirst is None
    bsz, seq, _ = x.shape
    row, full, win = _layer_specs(layer)
    tile = lambda w: pl.BlockSpec((1, tc, w), lambda b, j: (b, j, 0))
    in_specs = [tile(D_MODEL), row(D_MODEL), win(C_SHIFT_WIDTH, p["o_c"]), full(D_MODEL, LANES),
                win(C_WIDTH, p["o_cg"]), win(D_MODEL, p["o_mc"]), row(SHIFT_W),
                row(C_WIDTH), row(C_WIDTH), full(LANES, 2 * C_WIDTH),
                row(C_WIDTH), row(C_WIDTH), row(C_WIDTH), row(C_WIDTH), row(C_WIDTH),
                full(C_WIDTH, D_MODEL), row(C_WIDTH), full(LANES, C_WIDTH)]
    args = [x, p["norm_g"], p["w_in"], p["w_vres"], p["w_in"], p["w_in"], p["mu"], p["w0"], p["a0"], p["wlr"],
            p["kk"], p["ka"], p["rk"], p["gn_g"], p["gn_b"], p["wbc"], p["v0"], p["wv2"]]
    out_shape = [jax.ShapeDtypeStruct((bsz, seq, D_MODEL), F32)]
    out_specs = [tile(D_MODEL)]
    if first:
        out_shape.append(jax.ShapeDtypeStruct((bsz, seq, C_WIDTH), F32))
        out_specs.append(tile(C_WIDTH))
    else:
        in_specs.append(tile(C_WIDTH))
        args.append(vfirst)
    scratch = [pltpu.VMEM((N_GROUPS, GROUP, GROUP), F32),
               pltpu.VMEM((1, SHIFT_W), F32),
               pltpu.VMEM((tc, C_WIDTH), F32), pltpu.VMEM((tc, C_WIDTH), F32),
               pltpu.VMEM((tc, D_MODEL), F32)]
    return pl.pallas_call(
        functools.partial(_wkv_kernel, first, tc),
        grid=(bsz, seq // tc),
        in_specs=in_specs,
        out_specs=out_specs,
        out_shape=out_shape,
        scratch_shapes=scratch,
        compiler_params=pltpu.CompilerParams(
            dimension_semantics=("arbitrary", "arbitrary"), vmem_limit_bytes=VMEM_LIMIT),
        name="wkv_first" if first else "wkv",
    )(*args)


def _abx_kernel(last, tm, x_ref, mc_ref, ng_ref, wab_ref, wqm_ref, wmx_ref, alg_ref, alb_ref, aws_ref, abs_ref,
                bw_ref, bb_ref, blg_ref, blb_ref, ks_ref, vs_ref, wba_ref, wbb_ref, wbx_ref,
                wout_ref, fg_ref, out_ref, hist_ref):
    @pl.when(pl.program_id(1) == 0)
    def _():
        hist_ref[0:CONV_HIST, :] = jnp.zeros((CONV_HIST, B_WIDTH), F32)

    x = x_ref[0]
    h = _rmsnorm(x, ng_ref[0]).astype(BF16)
    cols = {id(wab_ref): 0, id(wqm_ref): 0, id(wmx_ref): 0}

    def proj(w_ref, width):
        start = cols[id(w_ref)]
        cols[id(w_ref)] = start + width
        return _dot(h, w_ref[0, :, start:start + width])

    u = _gelu_tanh(proj(wab_ref, A_WIDTH))
    vv = _layernorm(_gelu_tanh(proj(wab_ref, A_WIDTH)), alg_ref[0], alb_ref[0], LN_EPS)
    gate_a = _silu(proj(wab_ref, A_WIDTH))
    gw = A_WIDTH // A_GROUPS
    causal = (lax.broadcasted_iota(jnp.int32, (A_CHUNK, A_CHUNK), 0)
              >= lax.broadcasted_iota(jnp.int32, (A_CHUNK, A_CHUNK), 1))
    ws = [jnp.where(causal, aws_ref[0, g], 0.0).astype(BF16) for g in range(A_GROUPS)]
    vb = vv.astype(BF16)
    s_rows = []
    for n in range(tm // A_CHUNK):
        blk = vb[n * A_CHUNK:(n + 1) * A_CHUNK, :]
        s_rows.append(jnp.concatenate(
            [_dot(ws[g], blk[:, g * gw:(g + 1) * gw]) for g in range(A_GROUPS)], axis=1) + abs_ref[0])
    s = jnp.concatenate(s_rows, axis=0)
    ya = u * s * gate_a

    hb = proj(wab_ref, B_WIDTH)
    hb = hb * _sigmoid(proj(wab_ref, B_WIDTH))
    gate_b = _silu(proj(wab_ref, B_WIDTH))
    hist_ref[CONV_HIST:CONV_HIST + tm, :] = hb
    lead = CONV_HIST - (B_KERNEL - 1)
    acc = jnp.zeros((tm, B_WIDTH), F32) + bb_ref[0]
    for s8 in range(SUBLANES):
        n = tm if s8 == 0 else tm + SUBLANES
        part = None
        for q in range(CONV_HIST // SUBLANES + 1):
            t = SUBLANES * q + s8 - lead
            if 0 <= t < B_KERNEL:
                term = bw_ref[0, t:t + 1, :] * hist_ref[SUBLANES * q:SUBLANES * q + n, :]
                part = term if part is None else part + term
        acc = acc + part[s8:s8 + tm]
    hist_ref[0:CONV_HIST, :] = hist_ref[tm:tm + CONV_HIST, :]
    yb = _silu(_layernorm(acc, blg_ref[0], blb_ref[0], LN_EPS)) * gate_b

    qx = proj(wqm_ref, X_WIDTH)
    sc = _mm_nt(qx, ks_ref[0, 0]) * (X_HEAD ** -0.5)
    ps = []
    for hd in range(X_HEADS):
        sh = sc[:, hd * MEM_LEN:(hd + 1) * MEM_LEN]
        e = jnp.exp(sh - jnp.max(sh, axis=-1, keepdims=True))
        ps.append(e * (1.0 / jnp.sum(e, axis=-1, keepdims=True)))
    yx = _mm(jnp.concatenate(ps, axis=1), vs_ref[0, 0])

    merged = _sigmoid(proj(wqm_ref, D_MODEL)) * _mm(ya, wba_ref[0])
    merged = merged + _sigmoid(proj(wqm_ref, D_MODEL)) * _mm(yb, wbb_ref[0])
    merged = merged + _sigmoid(proj(wmx_ref, D_MODEL)) * _mm(yx, wbx_ref[0])
    merged = merged + mc_ref[0]
    y = x + _mm(merged, wout_ref[0])
    if last:
        y = _rmsnorm(y, fg_ref[...])
    out_ref[0] = y


def _abx_call(layer, last, tm, x, mc, p, kst, vst, fg):
    bsz, seq, _ = x.shape
    row, full, win = _layer_specs(layer)
    tile = lambda wd: pl.BlockSpec((1, tm, wd), lambda b, j: (b, j, 0))
    mem = pl.BlockSpec((1, 1, X_HEADS * MEM_LEN, X_WIDTH), lambda b, j: (layer, b, 0, 0))
    in_specs = [tile(D_MODEL), tile(D_MODEL), row(D_MODEL),
                win(3 * A_WIDTH + 3 * B_WIDTH, p["o_a"]), win(X_WIDTH + 2 * D_MODEL, p["o_x"]),
                win(D_MODEL, p["o_mx"]),
                row(A_WIDTH), row(A_WIDTH), full(A_GROUPS, A_CHUNK, A_CHUNK), full(A_CHUNK, A_WIDTH),
                full(CONV_HIST, B_WIDTH), row(B_WIDTH), row(B_WIDTH), row(B_WIDTH),
                mem, mem,
                full(A_WIDTH, D_MODEL), full(B_WIDTH, D_MODEL), full(X_WIDTH, D_MODEL),
                full(D_MODEL, D_MODEL), pl.BlockSpec((1, D_MODEL), lambda b, j: (0, 0))]
    args = [x, mc, p["norm_g"], p["w_in"], p["w_in"], p["w_in"], p["a_ln_g"], p["a_ln_b"], p["a_ws"], p["a_bs"],
            p["b_wdw"], p["b_bdw"], p["b_ln_g"], p["b_ln_b"], kst, vst,
            p["wba"], p["wbb"], p["wbx"], p["wout"], fg]
    return pl.pallas_call(
        functools.partial(_abx_kernel, last, tm),
        grid=(bsz, seq // tm),
        in_specs=in_specs,
        out_specs=tile(D_MODEL),
        out_shape=jax.ShapeDtypeStruct((bsz, seq, D_MODEL), F32),
        scratch_shapes=[pltpu.VMEM((CONV_HIST + tm, B_WIDTH), F32)],
        compiler_params=pltpu.CompilerParams(
            dimension_semantics=("arbitrary", "arbitrary"), vmem_limit_bytes=VMEM_LIMIT),
        name="abx_last" if last else "abx",
    )(*args)


def _pick_tile(seq, want, multiple):
    t = min(want, seq)
    while seq % t or t % multiple:
        t -= multiple
    return t


def kernel(x, mem, norm_g, w_in, w_in_vres, a_ln_g, a_ln_b, a_ws, a_bs, b_wdw, b_bdw, b_ln_g, b_ln_b, c_mu, c_mu_vres, c_w0, c_ww2, c_a0, c_wa2, c_v0, c_wv2, c_kk, c_ka, c_rk, c_gn_g, c_gn_b, mem_norm_g, w_mem_kv, w_branch_a, w_branch_b, w_branch_c, w_branch_x, w_out, final_norm_g):
    depth = w_in.shape[0]
    seq = x.shape[1]
    tc = _pick_tile(seq, 512, WKV_CHUNK)
    tm = _pick_tile(seq, 512, A_CHUNK)

    o_a = 0
    o_b = o_a + 3 * A_WIDTH
    o_c = o_b + 3 * B_WIDTH
    o_cg = o_c + C_SHIFT_WIDTH
    o_x = o_cg + C_WIDTH
    o_m = o_x + X_WIDTH

    rows = lambda a: a.reshape(depth, 1, -1)
    pad_to = lambda a, n, axis: jnp.pad(a, [(0, n - a.shape[i]) if i == axis else (0, 0) for i in range(a.ndim)])
    lead0 = lambda a: jnp.concatenate([jnp.zeros_like(a[:1]), a], axis=0)
    col_cat = lambda *a: jnp.concatenate(a, axis=2)

    w_in_bf = w_in.astype(BF16)
    zeros_lr = jnp.zeros_like(c_ww2)
    wkv_p = dict(
        norm_g=rows(norm_g), w_in=w_in_bf, o_c=o_c, o_cg=o_cg, o_mc=o_m + 2 * D_MODEL,
        w_vres=pad_to(lead0(w_in_vres), LANES, 2).astype(BF16),
        mu=rows(jnp.concatenate([c_mu, pad_to(lead0(c_mu_vres), LANES, 1)], axis=1)),
        w0=rows(c_w0), a0=rows(c_a0),
        wlr=jnp.concatenate([col_cat(c_ww2, zeros_lr), col_cat(zeros_lr, c_wa2)], axis=1).astype(BF16),
        kk=rows(c_kk), ka=rows(c_ka), rk=rows(c_rk), gn_g=rows(c_gn_g), gn_b=rows(c_gn_b),
        wbc=w_branch_c.astype(BF16),
        v0=rows(lead0(c_v0)), wv2=pad_to(lead0(c_wv2), LANES, 1).astype(BF16))
    abx_p = dict(
        norm_g=rows(norm_g), w_in=w_in_bf, o_a=o_a, o_x=o_x, o_mx=o_m + 3 * D_MODEL,
        a_ln_g=rows(a_ln_g), a_ln_b=rows(a_ln_b), a_ws=a_ws,
        a_bs=jnp.repeat(jnp.swapaxes(a_bs, 1, 2), A_WIDTH // A_GROUPS, axis=2),
        b_wdw=pad_to(b_wdw, CONV_HIST, 1), b_bdw=rows(b_bdw), b_ln_g=rows(b_ln_g), b_ln_b=rows(b_ln_b),
        wba=w_branch_a.astype(BF16), wbb=w_branch_b.astype(BF16), wbx=w_branch_x.astype(BF16),
        wout=w_out.astype(BF16))

    kst, vst = _memkv(mem, mem_norm_g, w_mem_kv)
    fg = final_norm_g.reshape(1, D_MODEL)
    v_first = None
    for i in range(depth):
        if i == 0:
            mc, v_first = _wkv_call(i, tc, x, wkv_p, None)
        else:
            (mc,) = _wkv_call(i, tc, x, wkv_p, v_first)
        x = _abx_call(i, i == depth - 1, tm, x, mc, abx_p, kst, vst, fg)
    return x
```

```python
import functools

import jax
import jax.numpy as jnp
from jax import lax
from jax.experimental import pallas as pl
from jax.experimental.pallas import tpu as pltpu

F32 = jnp.float32
BF16 = jnp.bfloat16

D_MODEL = 1024
MEM_LEN = 256
A_WIDTH = 512
A_GROUPS = 4
A_CHUNK = 128
B_WIDTH = 512
B_KERNEL = 31
C_HEAD = 64
C_HEADS = 8
C_WIDTH = C_HEADS * C_HEAD
C_DECAY_RANK = 64
C_AAA_RANK = 64
C_VRES_RANK = 32
C_SHIFT_WIDTH = 3 * C_WIDTH + C_DECAY_RANK + C_AAA_RANK
X_HEADS = 4
X_HEAD = 64
X_WIDTH = X_HEADS * X_HEAD

RMS_EPS = 1e-6
LN_EPS = 1e-5
GN_EPS = 64e-5
L2_EPS = 1e-12

LANES = 128
SUBLANES = 8
WKV_CHUNK = 64
WKV_MID = WKV_CHUNK // 2 - 1
GROUP = 2 * C_HEAD
N_GROUPS = C_WIDTH // GROUP
SHIFT_W = C_SHIFT_WIDTH + LANES
CONV_HIST = 32
VMEM_LIMIT = 56 * 1024 * 1024


def _split2(x):
    hi = x.astype(BF16)
    lo = (x - hi.astype(F32)).astype(BF16)
    return hi, lo


def _dot(a, b):
    return jnp.dot(a, b, preferred_element_type=F32)


def _mm(a, b):
    return _dot(a.astype(BF16), b.astype(BF16))


def _mm_nt(a, b):
    return lax.dot_general(a.astype(BF16), b.astype(BF16), (((1,), (1,)), ((), ())),
                           preferred_element_type=F32)


def _rmsnorm(x, g):
    ms = jnp.mean(x * x, axis=-1, keepdims=True)
    return x * lax.rsqrt(ms + RMS_EPS) * g


def _layernorm(x, g, b, eps):
    mu = jnp.mean(x, axis=-1, keepdims=True)
    xc = x - mu
    var = jnp.mean(xc * xc, axis=-1, keepdims=True)
    return xc * lax.rsqrt(var + eps) * g + b


def _sigmoid(x):
    return jax.nn.sigmoid(x)


def _silu(x):
    return x * jax.nn.sigmoid(x)


def _gelu_tanh(x):
    c = 0.7978845608028654
    return 0.5 * x * (1.0 + jnp.tanh(c * (x + 0.044715 * (x * x * x))))


def _head_stack(x, n_heads, head_width):
    lane_head = lax.broadcasted_iota(jnp.int32, x.shape, 1) // head_width
    zero = jnp.zeros_like(x)
    return jnp.concatenate([jnp.where(lane_head == h, x, zero) for h in range(n_heads)], axis=0)


def _shift_rows(z, prev_row):
    rolled = pltpu.roll(z, 1, 0)
    row = lax.broadcasted_iota(jnp.int32, z.shape, 0)
    return jnp.where(row == 0, prev_row, rolled)


def _memkv_kernel(mem_ref, g_ref, w_ref, k_ref, v_ref):
    mem_n = _rmsnorm(mem_ref[0], g_ref[...])
    kv = _mm(mem_n, w_ref[0])
    k_ref[0, 0] = _head_stack(kv[:, :X_WIDTH], X_HEADS, X_HEAD).astype(BF16)
    v_ref[0, 0] = _head_stack(kv[:, X_WIDTH:], X_HEADS, X_HEAD).astype(BF16)


def _memkv(mem, mem_norm_g, w_mem_kv):
    bsz = mem.shape[0]
    depth = w_mem_kv.shape[0]
    out = jax.ShapeDtypeStruct((depth, bsz, X_HEADS * MEM_LEN, X_WIDTH), BF16)
    spec = pl.BlockSpec((1, 1, X_HEADS * MEM_LEN, X_WIDTH), lambda l, b: (l, b, 0, 0))
    return pl.pallas_call(
        _memkv_kernel,
        grid=(depth, bsz),
        in_specs=[pl.BlockSpec((1, MEM_LEN, D_MODEL), lambda l, b: (b, 0, 0)),
                  pl.BlockSpec((1, D_MODEL), lambda l, b: (0, 0)),
                  pl.BlockSpec((1, D_MODEL, 2 * X_WIDTH), lambda l, b: (l, 0, 0))],
        out_specs=[spec, spec],
        out_shape=[out, out],
        name="memkv",
    )(mem, mem_norm_g.reshape(1, D_MODEL), w_mem_kv)


def _wkv_local(chains, between=lambda: None):
    c = WKV_CHUNK
    gw = GROUP
    nh = gw // C_HEAD
    stack = lambda z: _head_stack(z.astype(BF16), nh, C_HEAD)
    cat0 = lambda *zs: jnp.concatenate(zs, axis=0)
    cat1 = lambda *zs: jnp.concatenate(zs, axis=1)
    row = lax.broadcasted_iota(jnp.int32, (c, gw), 0)
    col = lax.broadcasted_iota(jnp.int32, (c, gw), 1) % c
    strict = row > col
    incl = row >= col
    eye = jnp.where(row == col, 1.0, 0.0)
    zero = jnp.zeros((c, gw), F32)
    rowk = lax.broadcasted_iota(jnp.int32, (gw, gw), 0)
    colk = lax.broadcasted_iota(jnp.int32, (gw, gw), 1)
    same_head = (rowk // C_HEAD) == (colk // C_HEAD)
    diag = rowk == colk
    zq = jnp.zeros((gw, gw), F32)

    prep = []
    for r, k, v, kk, b, lw, cl in chains:
        m = cl[WKV_MID:WKV_MID + 1, :]
        e_fwd = jnp.exp(cl - m)
        e_inv = jnp.exp(m - cl)
        e_mid = jnp.exp(m)
        e_end = jnp.exp(cl[c - 1:c, :] - m)
        r_t = r * e_fwd
        a_t = -kk * jnp.exp(cl - lw - m)
        b_t = b * e_inv
        k_t = k * e_inv
        prep.append(dict(
            v=v, r_t=r_t, a_t=a_t, b_t=b_t, k_t=k_t,
            r_bar=r_t * e_mid,
            a_bar=a_t * e_mid,
            bk_hat=cat1((b_t * e_end).T, (k_t * e_end).T).astype(BF16),
            d_end=e_end * e_mid))

    g = [_mm_nt(cat0(p["a_t"], p["r_t"]), cat0(stack(p["b_t"]), stack(p["k_t"]))) for p in prep]
    between()
    l_ab = [jnp.where(strict, x[:c, :gw], zero) for x in g]
    l_ak = [jnp.where(strict, x[:c, gw:], zero) for x in g]
    a_rb = [jnp.where(incl, x[c:, :gw], zero) for x in g]
    a_rk = [jnp.where(incl, x[c:, gw:], zero) for x in g]

    p_acc = [eye + x for x in l_ab]
    q_pow = [_mm(x, stack(x)) for x in l_ab]
    between()
    for _ in range(4):
        step = [_mm(cat0(p, q), stack(q)) for p, q in zip(p_acc, q_pow)]
        between()
        p_acc = [p + x[:c] for p, x in zip(p_acc, step)]
        q_pow = [x[c:] for x in step]
    t_inv = [p + _mm(p, stack(q)) for p, q in zip(p_acc, q_pow)]
    between()

    lv = [_mm(cat0(x, y), stack(p["v"])) for x, y, p in zip(l_ak, a_rk, prep)]
    tva = [_mm(t, cat1(stack(x[:c]), stack(p["a_bar"]))) for t, x, p in zip(t_inv, lv, prep)]
    tv = [x[:, :gw] for x in tva]
    a_p = [x[:, gw:] for x in tva]
    ro = [_mm(x, cat1(stack(y), stack(z))) for x, y, z in zip(a_rb, a_p, tv)]
    r_p = [(p["r_bar"] + x[:, :gw]).astype(BF16) for p, x in zip(prep, ro)]
    o_loc = [x[:, gw:] + y[c:] for x, y in zip(ro, lv)]
    m_full = [(jnp.where(diag, jnp.broadcast_to(p["d_end"], (gw, gw)), zq)
               + jnp.where(same_head, _mm(p["bk_hat"][:, :c], x), zq)).astype(BF16)
              for p, x in zip(prep, a_p)]
    n_full = [jnp.where(same_head, _mm(p["bk_hat"], cat0(x, p["v"])), zq) for p, x in zip(prep, tv)]
    return list(zip(r_p, o_loc, m_full, n_full))


def _wkv_kernel(first, tc, x_ref, ng_ref, wzc_ref, wvr_ref, wgate_ref, wmg_ref, mu_ref, w0_ref, a0_ref,
                wlr_ref, kk_ref, ka_ref, rk_ref, gng_ref, gnb_ref, wbc_ref, v0_ref, wv2_ref, *refs):
    if first:
        out_ref, vfirst_out_ref, state_ref, zlast_ref, o_s, gate_s, mg_s = refs
    else:
        vfirst_ref, out_ref, state_ref, zlast_ref, o_s, gate_s, mg_s = refs

    @pl.when(pl.program_id(1) == 0)
    def _():
        state_ref[...] = jnp.zeros_like(state_ref)
        zlast_ref[...] = jnp.zeros_like(zlast_ref)

    h = _rmsnorm(x_ref[0], ng_ref[0]).astype(BF16)
    zc = jnp.concatenate([_dot(h, wzc_ref[0]), _dot(h, wvr_ref[0])], axis=1)
    gate_s[...] = _dot(h, wgate_ref[0])
    mg_s[...] = _dot(h, wmg_ref[0])

    zs = _shift_rows(zc, zlast_ref[...])
    zlast_ref[...] = zc[tc - 1:tc, :]
    zc = zc + mu_ref[0] * (zs - zc)

    r = zc[:, :C_WIDTH]
    k = zc[:, C_WIDTH:2 * C_WIDTH]
    v = zc[:, 2 * C_WIDTH:3 * C_WIDTH]
    lo = zc[:, 3 * C_WIDTH:C_SHIFT_WIDTH]
    lane = lax.broadcasted_iota(jnp.int32, lo.shape, 1)
    lo = jnp.where(lane < C_DECAY_RANK, jnp.tanh(lo), lo)
    lr = _mm(lo, wlr_ref[0])
    lw = -0.6065306597126334 * _sigmoid(w0_ref[0] + lr[:, :C_WIDTH])
    a_sig = _sigmoid(a0_ref[0] + lr[:, C_WIDTH:])

    if first:
        vfirst_out_ref[0] = v
    else:
        zv = zc[:, C_SHIFT_WIDTH:]
        v_mix = _sigmoid(v0_ref[0] + _mm(zv, wv2_ref[0]))
        v = v + (vfirst_ref[0] - v) * v_mix

    sum_w = 2 * LANES
    rr = lax.broadcasted_iota(jnp.int32, (sum_w, sum_w), 0) // C_HEAD
    cc = lax.broadcasted_iota(jnp.int32, (sum_w, sum_w), 1) // C_HEAD
    head_ones = jnp.where(rr == cc, 1.0, 0.0).astype(BF16)

    def head_sum(z):
        return jnp.concatenate([_mm(z[:, i:i + sum_w], head_ones) for i in range(0, C_WIDTH, sum_w)], axis=1)

    kk = k * kk_ref[0]
    kk = kk * lax.rsqrt(head_sum(kk * kk) + L2_EPS)
    k = k * (1.0 + (a_sig - 1.0) * ka_ref[0])

    c = WKV_CHUNK
    n_chunks = tc // c
    tri = (lax.broadcasted_iota(jnp.int32, (c, c), 0)
           >= lax.broadcasted_iota(jnp.int32, (c, c), 1))
    tri = jnp.where(tri, 1.0, 0.0).astype(BF16)
    w_split = jnp.concatenate(_split2(lw), axis=1)
    cl = []
    for i in range(n_chunks):
        part = _dot(tri, w_split[i * c:(i + 1) * c, :])
        cl.append(part[:, :C_WIDTH] + part[:, C_WIDTH:])
    cl = jnp.concatenate(cl, axis=0)
    b_all = kk * a_sig
    sl = lambda z, i, q: z[i * c:(i + 1) * c, q * GROUP:(q + 1) * GROUP]
    hs = [state_ref[q] for q in range(N_GROUPS)]
    pending = []

    def carry_step():
        if pending:
            i, q, (r_p, o_loc, m_full, n_full) = pending.pop(0)
            both = _dot(jnp.concatenate([r_p, m_full], axis=0), hs[q].astype(BF16))
            o_s[i * c:(i + 1) * c, q * GROUP:(q + 1) * GROUP] = both[:c] + o_loc
            hs[q] = both[c:] + n_full

    def carry_chunk():
        for _ in range(N_GROUPS):
            carry_step()

    wave = max(1, n_chunks // 2)
    for w0 in range(0, n_chunks, wave):
        keys = [(i, q) for i in range(w0, min(w0 + wave, n_chunks)) for q in range(N_GROUPS)]
        local = _wkv_local([tuple(sl(z, i, q) for z in (r, k, v, kk, b_all, lw, cl)) for i, q in keys],
                           between=carry_chunk)
        pending.extend((i, q, m) for (i, q), m in zip(keys, local))
    while pending:
        carry_step()
    for q in range(N_GROUPS):
        state_ref[q] = hs[q]

    o = o_s[...]
    mu = head_sum(o) * (1.0 / C_HEAD)
    oc = o - mu
    var = head_sum(oc * oc) * (1.0 / C_HEAD)
    o = oc * lax.rsqrt(var + GN_EPS) * gng_ref[0] + gnb_ref[0]
    o = o + head_sum(r * k * rk_ref[0]) * v
    yc = o * _silu(gate_s[...])
    out_ref[0] = _sigmoid(mg_s[...]) * _mm(yc, wbc_ref[0])


def _layer_specs(layer):
    row = lambda w: pl.BlockSpec((1, 1, w), lambda b, j: (layer, 0, 0))
    full = lambda *s: pl.BlockSpec((1,) + s, lambda b, j: (layer,) + (0,) * len(s),
                                   pipeline_mode=pl.Buffered(1))
    win = lambda width, start: pl.BlockSpec((pl.Element(1), pl.Element(D_MODEL), pl.Element(width)),
                                            lambda b, j: (layer, 0, start), pipeline_mode=pl.Buffered(1))
    return row, full, win


def _wkv_call(layer, tc, x, p, vfirst):
    first = vfirst is None
    bsz, seq, _ = x.shape
    row, full, win = _layer_specs(layer)
    tile = lambda w: pl.BlockSpec((1, tc, w), lambda b, j: (b, j, 0))
    in_specs = [tile(D_MODEL), row(D_MODEL), win(C_SHIFT_WIDTH, p["o_c"]), full(D_MODEL, LANES),
                win(C_WIDTH, p["o_cg"]), win(D_MODEL, p["o_mc"]), row(SHIFT_W),
                row(C_WIDTH), row(C_WIDTH), full(LANES, 2 * C_WIDTH),
                row(C_WIDTH), row(C_WIDTH), row(C_WIDTH), row(C_WIDTH), row(C_WIDTH),
                full(C_WIDTH, D_MODEL), row(C_WIDTH), full(LANES, C_WIDTH)]
    args = [x, p["norm_g"], p["w_in"], p["w_vres"], p["w_in"], p["w_in"], p["mu"], p["w0"], p["a0"], p["wlr"],
            p["kk"], p["ka"], p["rk"], p["gn_g"], p["gn_b"], p["wbc"], p["v0"], p["wv2"]]
    out_shape = [jax.ShapeDtypeStruct((bsz, seq, D_MODEL), F32)]
    out_specs = [tile(D_MODEL)]
    if first:
        out_shape.append(jax.ShapeDtypeStruct((bsz, seq, C_WIDTH), F32))
        out_specs.append(tile(C_WIDTH))
    else:
        in_specs.append(tile(C_WIDTH))
        args.append(vfirst)
    scratch = [pltpu.VMEM((N_GROUPS, GROUP, GROUP), F32),
               pltpu.VMEM((1, SHIFT_W), F32),
               pltpu.VMEM((tc, C_WIDTH), F32), pltpu.VMEM((tc, C_WIDTH), F32),
               pltpu.VMEM((tc, D_MODEL), F32)]
    return pl.pallas_call(
        functools.partial(_wkv_kernel, first, tc),
        grid=(bsz, seq // tc),
        in_specs=in_specs,
        out_specs=out_specs,
        out_shape=out_shape,
        scratch_shapes=scratch,
        compiler_params=pltpu.CompilerParams(
            dimension_semantics=("arbitrary", "arbitrary"), vmem_limit_bytes=VMEM_LIMIT),
        name="wkv_first" if first else "wkv",
    )(*args)


def _abx_kernel(last, tm, x_ref, mc_ref, ng_ref, wab_ref, wqm_ref, wmx_ref, alg_ref, alb_ref, aws_ref, abs_ref,
                bw_ref, bb_ref, blg_ref, blb_ref, ks_ref, vs_ref, wba_ref, wbb_ref, wbx_ref,
                wout_ref, fg_ref, out_ref, hist_ref, zm_s):
    @pl.when(pl.program_id(1) == 0)
    def _():
        hist_ref[0:CONV_HIST, :] = jnp.zeros((CONV_HIST, B_WIDTH), F32)

    x = x_ref[0]
    h = _rmsnorm(x, ng_ref[0]).astype(BF16)
    cols = {id(wab_ref): 0, id(wqm_ref): 0, id(wmx_ref): 0}

    def proj(w_ref, width):
        start = cols[id(w_ref)]
        cols[id(w_ref)] = start + width
        return _dot(h, w_ref[0, :, start:start + width])

    u = _gelu_tanh(proj(wab_ref, A_WIDTH))
    vv = _layernorm(_gelu_tanh(proj(wab_ref, A_WIDTH)), alg_ref[0], alb_ref[0], LN_EPS)
    gate_a = _silu(proj(wab_ref, A_WIDTH))
    gw = A_WIDTH // A_GROUPS
    causal = (lax.broadcasted_iota(jnp.int32, (A_CHUNK, A_CHUNK), 0)
              >= lax.broadcasted_iota(jnp.int32, (A_CHUNK, A_CHUNK), 1))
    ws = [jnp.where(causal, aws_ref[0, g], 0.0).astype(BF16) for g in range(A_GROUPS)]
    vb = vv.astype(BF16)
    s_rows = []
    for n in range(tm // A_CHUNK):
        blk = vb[n * A_CHUNK:(n + 1) * A_CHUNK, :]
        s_rows.append(jnp.concatenate(
            [_dot(ws[g], blk[:, g * gw:(g + 1) * gw]) for g in range(A_GROUPS)], axis=1) + abs_ref[0])
    s = jnp.concatenate(s_rows, axis=0)
    ya = u * s * gate_a

    hb = proj(wab_ref, B_WIDTH)
    hb = hb * _sigmoid(proj(wab_ref, B_WIDTH))
    gate_b = _silu(proj(wab_ref, B_WIDTH))
    hist_ref[CONV_HIST:CONV_HIST + tm, :] = hb
    zm_s[:, :2 * D_MODEL] = _dot(h, wqm_ref[0, :, X_WIDTH:])
    zm_s[:, 2 * D_MODEL:] = _dot(h, wmx_ref[0])
    lead = CONV_HIST - (B_KERNEL - 1)
    acc = jnp.zeros((tm, B_WIDTH), F32) + bb_ref[0]
    for s8 in range(SUBLANES):
        n = tm if s8 == 0 else tm + SUBLANES
        part = None
        for q in range(CONV_HIST // SUBLANES + 1):
            t = SUBLANES * q + s8 - lead
            if 0 <= t < B_KERNEL:
                term = bw_ref[0, t:t + 1, :] * hist_ref[SUBLANES * q:SUBLANES * q + n, :]
                part = term if part is None else part + term
        acc = acc + part[s8:s8 + tm]
    hist_ref[0:CONV_HIST, :] = hist_ref[tm:tm + CONV_HIST, :]
    yb = _silu(_layernorm(acc, blg_ref[0], blb_ref[0], LN_EPS)) * gate_b

    qx = proj(wqm_ref, X_WIDTH)
    sc = _mm_nt(qx, ks_ref[0, 0]) * (X_HEAD ** -0.5)
    ps = []
    for hd in range(X_HEADS):
        sh = sc[:, hd * MEM_LEN:(hd + 1) * MEM_LEN]
        e = jnp.exp(sh - jnp.max(sh, axis=-1, keepdims=True))
        ps.append(e * (1.0 / jnp.sum(e, axis=-1, keepdims=True)))
    yx = _mm(jnp.concatenate(ps, axis=1), vs_ref[0, 0])

    merged = _sigmoid(zm_s[:, :D_MODEL]) * _mm(ya, wba_ref[0])
    merged = merged + _sigmoid(zm_s[:, D_MODEL:2 * D_MODEL]) * _mm(yb, wbb_ref[0])
    merged = merged + _sigmoid(zm_s[:, 2 * D_MODEL:]) * _mm(yx, wbx_ref[0])
    merged = merged + mc_ref[0]
    y = x + _mm(merged, wout_ref[0])
    if last:
        y = _rmsnorm(y, fg_ref[...])
    out_ref[0] = y


def _abx_call(layer, last, tm, x, mc, p, kst, vst, fg):
    bsz, seq, _ = x.shape
    row, full, win = _layer_specs(layer)
    tile = lambda wd: pl.BlockSpec((1, tm, wd), lambda b, j: (b, j, 0))
    mem = pl.BlockSpec((1, 1, X_HEADS * MEM_LEN, X_WIDTH), lambda b, j: (layer, b, 0, 0))
    in_specs = [tile(D_MODEL), tile(D_MODEL), row(D_MODEL),
                win(3 * A_WIDTH + 3 * B_WIDTH, p["o_a"]), win(X_WIDTH + 2 * D_MODEL, p["o_x"]),
                win(D_MODEL, p["o_mx"]),
                row(A_WIDTH), row(A_WIDTH), full(A_GROUPS, A_CHUNK, A_CHUNK), full(A_CHUNK, A_WIDTH),
                full(CONV_HIST, B_WIDTH), row(B_WIDTH), row(B_WIDTH), row(B_WIDTH),
                mem, mem,
                full(A_WIDTH, D_MODEL), full(B_WIDTH, D_MODEL), full(X_WIDTH, D_MODEL),
                full(D_MODEL, D_MODEL), pl.BlockSpec((1, D_MODEL), lambda b, j: (0, 0))]
    args = [x, mc, p["norm_g"], p["w_in"], p["w_in"], p["w_in"], p["a_ln_g"], p["a_ln_b"], p["a_ws"], p["a_bs"],
            p["b_wdw"], p["b_bdw"], p["b_ln_g"], p["b_ln_b"], kst, vst,
            p["wba"], p["wbb"], p["wbx"], p["wout"], fg]
    return pl.pallas_call(
        functools.partial(_abx_kernel, last, tm),
        grid=(bsz, seq // tm),
        in_specs=in_specs,
        out_specs=tile(D_MODEL),
        out_shape=jax.ShapeDtypeStruct((bsz, seq, D_MODEL), F32),
        scratch_shapes=[pltpu.VMEM((CONV_HIST + tm, B_WIDTH), F32), pltpu.VMEM((tm, 3 * D_MODEL), F32)],
        compiler_params=pltpu.CompilerParams(
            dimension_semantics=("arbitrary", "arbitrary"), vmem_limit_bytes=VMEM_LIMIT),
        name="abx_last" if last else "abx",
    )(*args)


def _pick_tile(seq, want, multiple):
    t = min(want, seq)
    while seq % t or t % multiple:
        t -= multiple
    return t


def kernel(x, mem, norm_g, w_in, w_in_vres, a_ln_g, a_ln_b, a_ws, a_bs, b_wdw, b_bdw, b_ln_g, b_ln_b, c_mu, c_mu_vres, c_w0, c_ww2, c_a0, c_wa2, c_v0, c_wv2, c_kk, c_ka, c_rk, c_gn_g, c_gn_b, mem_norm_g, w_mem_kv, w_branch_a, w_branch_b, w_branch_c, w_branch_x, w_out, final_norm_g):
    depth = w_in.shape[0]
    seq = x.shape[1]
    tc = _pick_tile(seq, 512, WKV_CHUNK)
    tm = _pick_tile(seq, 512, A_CHUNK)

    o_a = 0
    o_b = o_a + 3 * A_WIDTH
    o_c = o_b + 3 * B_WIDTH
    o_cg = o_c + C_SHIFT_WIDTH
    o_x = o_cg + C_WIDTH
    o_m = o_x + X_WIDTH

    rows = lambda a: a.reshape(depth, 1, -1)
    pad_to = lambda a, n, axis: jnp.pad(a, [(0, n - a.shape[i]) if i == axis else (0, 0) for i in range(a.ndim)])
    lead0 = lambda a: jnp.concatenate([jnp.zeros_like(a[:1]), a], axis=0)
    col_cat = lambda *a: jnp.concatenate(a, axis=2)

    w_in_bf = w_in.astype(BF16)
    zeros_lr = jnp.zeros_like(c_ww2)
    wkv_p = dict(
        norm_g=rows(norm_g), w_in=w_in_bf, o_c=o_c, o_cg=o_cg, o_mc=o_m + 2 * D_MODEL,
        w_vres=pad_to(lead0(w_in_vres), LANES, 2).astype(BF16),
        mu=rows(jnp.concatenate([c_mu, pad_to(lead0(c_mu_vres), LANES, 1)], axis=1)),
        w0=rows(c_w0), a0=rows(c_a0),
        wlr=jnp.concatenate([col_cat(c_ww2, zeros_lr), col_cat(zeros_lr, c_wa2)], axis=1).astype(BF16),
        kk=rows(c_kk), ka=rows(c_ka), rk=rows(c_rk), gn_g=rows(c_gn_g), gn_b=rows(c_gn_b),
        wbc=w_branch_c.astype(BF16),
        v0=rows(lead0(c_v0)), wv2=pad_to(lead0(c_wv2), LANES, 1).astype(BF16))
    abx_p = dict(
        norm_g=rows(norm_g), w_in=w_in_bf, o_a=o_a, o_x=o_x, o_mx=o_m + 3 * D_MODEL,
        a_ln_g=rows(a_ln_g), a_ln_b=rows(a_ln_b), a_ws=a_ws,
        a_bs=jnp.repeat(jnp.swapaxes(a_bs, 1, 2), A_WIDTH // A_GROUPS, axis=2),
        b_wdw=pad_to(b_wdw, CONV_HIST, 1), b_bdw=rows(b_bdw), b_ln_g=rows(b_ln_g), b_ln_b=rows(b_ln_b),
        wba=w_branch_a.astype(BF16), wbb=w_branch_b.astype(BF16), wbx=w_branch_x.astype(BF16),
        wout=w_out.astype(BF16))

    kst, vst = _memkv(mem, mem_norm_g, w_mem_kv)
    fg = final_norm_g.reshape(1, D_MODEL)
    v_first = None
    for i in range(depth):
        if i == 0:
            mc, v_first = _wkv_call(i, tc, x, wkv_p, None)
        else:
            (mc,) = _wkv_call(i, tc, x, wkv_p, v_first)
        x = _abx_call(i, i == depth - 1, tm, x, mc, abx_p, kst, vst, fg)
    return x
```

```python
import functools

import jax
import jax.numpy as jnp
from jax import lax
from jax.experimental import pallas as pl
from jax.experimental.pallas import tpu as pltpu

F32 = jnp.float32
BF16 = jnp.bfloat16

D_MODEL = 1024
MEM_LEN = 256
A_WIDTH = 512
A_GROUPS = 4
A_CHUNK = 128
B_WIDTH = 512
B_KERNEL = 31
C_HEAD = 64
C_HEADS = 8
C_WIDTH = C_HEADS * C_HEAD
C_DECAY_RANK = 64
C_AAA_RANK = 64
C_VRES_RANK = 32
C_SHIFT_WIDTH = 3 * C_WIDTH + C_DECAY_RANK + C_AAA_RANK
X_HEADS = 4
X_HEAD = 64
X_WIDTH = X_HEADS * X_HEAD

RMS_EPS = 1e-6
LN_EPS = 1e-5
GN_EPS = 64e-5
L2_EPS = 1e-12

LANES = 128
SUBLANES = 8
WKV_CHUNK = 64
WKV_MID = WKV_CHUNK // 2 - 1
GROUP = 2 * C_HEAD
N_GROUPS = C_WIDTH // GROUP
SHIFT_W = C_SHIFT_WIDTH + LANES
CONV_HIST = 32
VMEM_LIMIT = 56 * 1024 * 1024


def _split2(x):
    hi = x.astype(BF16)
    lo = (x - hi.astype(F32)).astype(BF16)
    return hi, lo


def _dot(a, b):
    return jnp.dot(a, b, preferred_element_type=F32)


def _mm(a, b):
    return _dot(a.astype(BF16), b.astype(BF16))


def _mm_nt(a, b):
    return lax.dot_general(a.astype(BF16), b.astype(BF16), (((1,), (1,)), ((), ())),
                           preferred_element_type=F32)


def _rmsnorm(x, g):
    ms = jnp.mean(x * x, axis=-1, keepdims=True)
    return x * lax.rsqrt(ms + RMS_EPS) * g


def _layernorm(x, g, b, eps):
    mu = jnp.mean(x, axis=-1, keepdims=True)
    xc = x - mu
    var = jnp.mean(xc * xc, axis=-1, keepdims=True)
    return xc * lax.rsqrt(var + eps) * g + b


def _sigmoid(x):
    return jax.nn.sigmoid(x)


def _silu(x):
    return x * jax.nn.sigmoid(x)


def _gelu_tanh(x):
    c = 0.7978845608028654
    return 0.5 * x * (1.0 + jnp.tanh(c * (x + 0.044715 * (x * x * x))))


def _head_stack(x, n_heads, head_width):
    lane_head = lax.broadcasted_iota(jnp.int32, x.shape, 1) // head_width
    zero = jnp.zeros_like(x)
    return jnp.concatenate([jnp.where(lane_head == h, x, zero) for h in range(n_heads)], axis=0)


def _shift_rows(z, prev_row):
    rolled = pltpu.roll(z, 1, 0)
    row = lax.broadcasted_iota(jnp.int32, z.shape, 0)
    return jnp.where(row == 0, prev_row, rolled)


def _memkv_kernel(mem_ref, g_ref, w_ref, k_ref, v_ref):
    mem_n = _rmsnorm(mem_ref[0], g_ref[...])
    kv = _mm(mem_n, w_ref[0])
    k_ref[0, 0] = _head_stack(kv[:, :X_WIDTH], X_HEADS, X_HEAD).astype(BF16)
    v_ref[0, 0] = _head_stack(kv[:, X_WIDTH:], X_HEADS, X_HEAD).astype(BF16)


def _memkv(mem, mem_norm_g, w_mem_kv):
    bsz = mem.shape[0]
    depth = w_mem_kv.shape[0]
    out = jax.ShapeDtypeStruct((depth, bsz, X_HEADS * MEM_LEN, X_WIDTH), BF16)
    spec = pl.BlockSpec((1, 1, X_HEADS * MEM_LEN, X_WIDTH), lambda l, b: (l, b, 0, 0))
    return pl.pallas_call(
        _memkv_kernel,
        grid=(depth, bsz),
        in_specs=[pl.BlockSpec((1, MEM_LEN, D_MODEL), lambda l, b: (b, 0, 0)),
                  pl.BlockSpec((1, D_MODEL), lambda l, b: (0, 0)),
                  pl.BlockSpec((1, D_MODEL, 2 * X_WIDTH), lambda l, b: (l, 0, 0))],
        out_specs=[spec, spec],
        out_shape=[out, out],
        name="memkv",
    )(mem, mem_norm_g.reshape(1, D_MODEL), w_mem_kv)


def _wkv_local(chains, between=lambda: None):
    c = WKV_CHUNK
    gw = GROUP
    nh = gw // C_HEAD
    stack = lambda z: _head_stack(z.astype(BF16), nh, C_HEAD)
    cat0 = lambda *zs: jnp.concatenate(zs, axis=0)
    cat1 = lambda *zs: jnp.concatenate(zs, axis=1)
    row = lax.broadcasted_iota(jnp.int32, (c, gw), 0)
    col = lax.broadcasted_iota(jnp.int32, (c, gw), 1) % c
    strict = row > col
    incl = row >= col
    eye = jnp.where(row == col, 1.0, 0.0)
    zero = jnp.zeros((c, gw), F32)
    rowk = lax.broadcasted_iota(jnp.int32, (gw, gw), 0)
    colk = lax.broadcasted_iota(jnp.int32, (gw, gw), 1)
    same_head = (rowk // C_HEAD) == (colk // C_HEAD)
    diag = rowk == colk
    zq = jnp.zeros((gw, gw), F32)

    prep = []
    for r, k, v, kk, b, lw, cl in chains:
        m = cl[WKV_MID:WKV_MID + 1, :]
        e_fwd = jnp.exp(cl - m)
        e_inv = jnp.exp(m - cl)
        e_mid = jnp.exp(m)
        e_end = jnp.exp(cl[c - 1:c, :] - m)
        r_t = r * e_fwd
        a_t = -kk * jnp.exp(cl - lw - m)
        b_t = b * e_inv
        k_t = k * e_inv
        prep.append(dict(
            v=v, r_t=r_t, a_t=a_t, b_t=b_t, k_t=k_t,
            r_bar=r_t * e_mid,
            a_bar=a_t * e_mid,
            bk_hat=cat1((b_t * e_end).T, (k_t * e_end).T).astype(BF16),
            d_end=e_end * e_mid))

    g = [_mm_nt(cat0(p["a_t"], p["r_t"]), cat0(stack(p["b_t"]), stack(p["k_t"]))) for p in prep]
    between()
    l_ab = [jnp.where(strict, x[:c, :gw], zero) for x in g]
    l_ak = [jnp.where(strict, x[:c, gw:], zero) for x in g]
    a_rb = [jnp.where(incl, x[c:, :gw], zero) for x in g]
    a_rk = [jnp.where(incl, x[c:, gw:], zero) for x in g]

    p_acc = [eye + x for x in l_ab]
    q_pow = [_mm(x, stack(x)) for x in l_ab]
    between()
    for _ in range(4):
        step = [_mm(cat0(p, q), stack(q)) for p, q in zip(p_acc, q_pow)]
        between()
        p_acc = [p + x[:c] for p, x in zip(p_acc, step)]
        q_pow = [x[c:] for x in step]
    t_inv = [p + _mm(p, stack(q)) for p, q in zip(p_acc, q_pow)]
    between()

    lv = [_mm(cat0(x, y), stack(p["v"])) for x, y, p in zip(l_ak, a_rk, prep)]
    tva = [_mm(t, cat1(stack(x[:c]), stack(p["a_bar"]))) for t, x, p in zip(t_inv, lv, prep)]
    tv = [x[:, :gw] for x in tva]
    a_p = [x[:, gw:] for x in tva]
    ro = [_mm(x, cat1(stack(y), stack(z))) for x, y, z in zip(a_rb, a_p, tv)]
    r_p = [(p["r_bar"] + x[:, :gw]).astype(BF16) for p, x in zip(prep, ro)]
    o_loc = [x[:, gw:] + y[c:] for x, y in zip(ro, lv)]
    m_full = [(jnp.where(diag, jnp.broadcast_to(p["d_end"], (gw, gw)), zq)
               + jnp.where(same_head, _mm(p["bk_hat"][:, :c], x), zq)).astype(BF16)
              for p, x in zip(prep, a_p)]
    n_full = [jnp.where(same_head, _mm(p["bk_hat"], cat0(x, p["v"])), zq) for p, x in zip(prep, tv)]
    return list(zip(r_p, o_loc, m_full, n_full))


def _wkv_kernel(first, tc, x_ref, ng_ref, wzc_ref, wvr_ref, wgate_ref, wmg_ref, mu_ref, w0_ref, a0_ref,
                wlr_ref, kk_ref, ka_ref, rk_ref, gng_ref, gnb_ref, wbc_ref, v0_ref, wv2_ref, *refs):
    if first:
        out_ref, vfirst_out_ref, state_ref, zlast_ref, o_s, gate_s, mg_s = refs
    else:
        vfirst_ref, out_ref, state_ref, zlast_ref, o_s, gate_s, mg_s = refs

    @pl.when(pl.program_id(1) == 0)
    def _():
        state_ref[...] = jnp.zeros_like(state_ref)
        zlast_ref[...] = jnp.zeros_like(zlast_ref)

    h = _rmsnorm(x_ref[0], ng_ref[0]).astype(BF16)
    zc = jnp.concatenate([_dot(h, wzc_ref[0]), _dot(h, wvr_ref[0])], axis=1)
    gate_s[...] = _dot(h, wgate_ref[0])

    zs = _shift_rows(zc, zlast_ref[...])
    zlast_ref[...] = zc[tc - 1:tc, :]
    zc = zc + mu_ref[0] * (zs - zc)

    r = zc[:, :C_WIDTH]
    k = zc[:, C_WIDTH:2 * C_WIDTH]
    v = zc[:, 2 * C_WIDTH:3 * C_WIDTH]
    lo = zc[:, 3 * C_WIDTH:C_SHIFT_WIDTH]
    lane = lax.broadcasted_iota(jnp.int32, lo.shape, 1)
    lo = jnp.where(lane < C_DECAY_RANK, jnp.tanh(lo), lo)
    lr = _mm(lo, wlr_ref[0])
    lw = -0.6065306597126334 * _sigmoid(w0_ref[0] + lr[:, :C_WIDTH])
    a_sig = _sigmoid(a0_ref[0] + lr[:, C_WIDTH:])

    if first:
        vfirst_out_ref[0] = v
    else:
        zv = zc[:, C_SHIFT_WIDTH:]
        v_mix = _sigmoid(v0_ref[0] + _mm(zv, wv2_ref[0]))
        v = v + (vfirst_ref[0] - v) * v_mix

    sum_w = 2 * LANES
    rr = lax.broadcasted_iota(jnp.int32, (sum_w, sum_w), 0) // C_HEAD
    cc = lax.broadcasted_iota(jnp.int32, (sum_w, sum_w), 1) // C_HEAD
    head_ones = jnp.where(rr == cc, 1.0, 0.0).astype(BF16)

    def head_sum(z):
        return jnp.concatenate([_mm(z[:, i:i + sum_w], head_ones) for i in range(0, C_WIDTH, sum_w)], axis=1)

    kk = k * kk_ref[0]
    kk = kk * lax.rsqrt(head_sum(kk * kk) + L2_EPS)
    k = k * (1.0 + (a_sig - 1.0) * ka_ref[0])

    c = WKV_CHUNK
    n_chunks = tc // c
    tri = (lax.broadcasted_iota(jnp.int32, (c, c), 0)
           >= lax.broadcasted_iota(jnp.int32, (c, c), 1))
    tri = jnp.where(tri, 1.0, 0.0).astype(BF16)
    w_split = jnp.concatenate(_split2(lw), axis=1)
    cl = []
    for i in range(n_chunks):
        part = _dot(tri, w_split[i * c:(i + 1) * c, :])
        cl.append(part[:, :C_WIDTH] + part[:, C_WIDTH:])
    cl = jnp.concatenate(cl, axis=0)
    b_all = kk * a_sig
    mg_s[...] = _dot(h, wmg_ref[0])
    sl = lambda z, i, q: z[i * c:(i + 1) * c, q * GROUP:(q + 1) * GROUP]
    hs = [state_ref[q] for q in range(N_GROUPS)]
    pending = []

    def carry_step():
        if pending:
            i, q, (r_p, o_loc, m_full, n_full) = pending.pop(0)
            both = _dot(jnp.concatenate([r_p, m_full], axis=0), hs[q].astype(BF16))
            o_s[i * c:(i + 1) * c, q * GROUP:(q + 1) * GROUP] = both[:c] + o_loc
            hs[q] = both[c:] + n_full

    def carry_chunk():
        for _ in range(N_GROUPS):
            carry_step()

    wave = max(1, n_chunks // 2)
    for w0 in range(0, n_chunks, wave):
        keys = [(i, q) for i in range(w0, min(w0 + wave, n_chunks)) for q in range(N_GROUPS)]
        local = _wkv_local([tuple(sl(z, i, q) for z in (r, k, v, kk, b_all, lw, cl)) for i, q in keys],
                           between=carry_chunk)
        pending.extend((i, q, m) for (i, q), m in zip(keys, local))
    while pending:
        carry_step()
    for q in range(N_GROUPS):
        state_ref[q] = hs[q]

    o = o_s[...]
    mu = head_sum(o) * (1.0 / C_HEAD)
    oc = o - mu
    var = head_sum(oc * oc) * (1.0 / C_HEAD)
    o = oc * lax.rsqrt(var + GN_EPS) * gng_ref[0] + gnb_ref[0]
    o = o + head_sum(r * k * rk_ref[0]) * v
    yc = o * _silu(gate_s[...])
    out_ref[0] = _sigmoid(mg_s[...]) * _mm(yc, wbc_ref[0])


def _layer_specs(layer):
    row = lambda w: pl.BlockSpec((1, 1, w), lambda b, j: (layer, 0, 0))
    full = lambda *s: pl.BlockSpec((1,) + s, lambda b, j: (layer,) + (0,) * len(s),
                                   pipeline_mode=pl.Buffered(1))
    win = lambda width, start: pl.BlockSpec((pl.Element(1), pl.Element(D_MODEL), pl.Element(width)),
                                            lambda b, j: (layer, 0, start), pipeline_mode=pl.Buffered(1))
    return row, full, win


def _wkv_call(layer, tc, x, p, vfirst):
    first = vfirst is None
    bsz, seq, _ = x.shape
    row, full, win = _layer_specs(layer)
    tile = lambda w: pl.BlockSpec((1, tc, w), lambda b, j: (b, j, 0))
    in_specs = [tile(D_MODEL), row(D_MODEL), win(C_SHIFT_WIDTH, p["o_c"]), full(D_MODEL, LANES),
                win(C_WIDTH, p["o_cg"]), win(D_MODEL, p["o_mc"]), row(SHIFT_W),
                row(C_WIDTH), row(C_WIDTH), full(LANES, 2 * C_WIDTH),
                row(C_WIDTH), row(C_WIDTH), row(C_WIDTH), row(C_WIDTH), row(C_WIDTH),
                full(C_WIDTH, D_MODEL), row(C_WIDTH), full(LANES, C_WIDTH)]
    args = [x, p["norm_g"], p["w_in"], p["w_vres"], p["w_in"], p["w_in"], p["mu"], p["w0"], p["a0"], p["wlr"],
            p["kk"], p["ka"], p["rk"], p["gn_g"], p["gn_b"], p["wbc"], p["v0"], p["wv2"]]
    out_shape = [jax.ShapeDtypeStruct((bsz, seq, D_MODEL), F32)]
    out_specs = [tile(D_MODEL)]
    if first:
        out_shape.append(jax.ShapeDtypeStruct((bsz, seq, C_WIDTH), F32))
        out_specs.append(tile(C_WIDTH))
    else:
        in_specs.append(tile(C_WIDTH))
        args.append(vfirst)
    scratch = [pltpu.VMEM((N_GROUPS, GROUP, GROUP), F32),
               pltpu.VMEM((1, SHIFT_W), F32),
               pltpu.VMEM((tc, C_WIDTH), F32), pltpu.VMEM((tc, C_WIDTH), F32),
               pltpu.VMEM((tc, D_MODEL), F32)]
    return pl.pallas_call(
        functools.partial(_wkv_kernel, first, tc),
        grid=(bsz, seq // tc),
        in_specs=in_specs,
        out_specs=out_specs,
        out_shape=out_shape,
        scratch_shapes=scratch,
        compiler_params=pltpu.CompilerParams(
            dimension_semantics=("arbitrary", "arbitrary"), vmem_limit_bytes=VMEM_LIMIT),
        name="wkv_first" if first else "wkv",
    )(*args)


def _abx_kernel(last, tm, x_ref, mc_ref, ng_ref, wab_ref, wqm_ref, wmx_ref, alg_ref, alb_ref, aws_ref, abs_ref,
                bw_ref, bb_ref, blg_ref, blb_ref, ks_ref, vs_ref, wba_ref, wbb_ref, wbx_ref,
                wout_ref, fg_ref, out_ref, hist_ref, zm_s):
    @pl.when(pl.program_id(1) == 0)
    def _():
        hist_ref[0:CONV_HIST, :] = jnp.zeros((CONV_HIST, B_WIDTH), F32)

    x = x_ref[0]
    h = _rmsnorm(x, ng_ref[0]).astype(BF16)
    cols = {id(wab_ref): 0, id(wqm_ref): 0, id(wmx_ref): 0}

    def proj(w_ref, width):
        start = cols[id(w_ref)]
        cols[id(w_ref)] = start + width
        return _dot(h, w_ref[0, :, start:start + width])

    u = _gelu_tanh(proj(wab_ref, A_WIDTH))
    vv = _layernorm(_gelu_tanh(proj(wab_ref, A_WIDTH)), alg_ref[0], alb_ref[0], LN_EPS)
    gate_a = _silu(proj(wab_ref, A_WIDTH))
    gw = A_WIDTH // A_GROUPS
    causal = (lax.broadcasted_iota(jnp.int32, (A_CHUNK, A_CHUNK), 0)
              >= lax.broadcasted_iota(jnp.int32, (A_CHUNK, A_CHUNK), 1))
    ws = [jnp.where(causal, aws_ref[0, g], 0.0).astype(BF16) for g in range(A_GROUPS)]
    vb = vv.astype(BF16)
    s_rows = []
    for n in range(tm // A_CHUNK):
        blk = vb[n * A_CHUNK:(n + 1) * A_CHUNK, :]
        s_rows.append(jnp.concatenate(
            [_dot(ws[g], blk[:, g * gw:(g + 1) * gw]) for g in range(A_GROUPS)], axis=1) + abs_ref[0])
    s = jnp.concatenate(s_rows, axis=0)
    ya = u * s * gate_a

    hb = proj(wab_ref, B_WIDTH)
    hb = hb * _sigmoid(proj(wab_ref, B_WIDTH))
    gate_b = _silu(proj(wab_ref, B_WIDTH))
    hist_ref[CONV_HIST:CONV_HIST + tm, :] = hb
    zm_s[:, :2 * D_MODEL] = _dot(h, wqm_ref[0, :, X_WIDTH:])
    zm_s[:, 2 * D_MODEL:] = _dot(h, wmx_ref[0])
    lead = CONV_HIST - (B_KERNEL - 1)
    acc = jnp.zeros((tm, B_WIDTH), F32) + bb_ref[0]
    for s8 in range(SUBLANES):
        n = tm if s8 == 0 else tm + SUBLANES
        part = None
        for q in range(CONV_HIST // SUBLANES + 1):
            t = SUBLANES * q + s8 - lead
            if 0 <= t < B_KERNEL:
                term = bw_ref[0, t:t + 1, :] * hist_ref[SUBLANES * q:SUBLANES * q + n, :]
                part = term if part is None else part + term
        acc = acc + part[s8:s8 + tm]
    hist_ref[0:CONV_HIST, :] = hist_ref[tm:tm + CONV_HIST, :]
    yb = _silu(_layernorm(acc, blg_ref[0], blb_ref[0], LN_EPS)) * gate_b

    qx = proj(wqm_ref, X_WIDTH)
    sc = _mm_nt(qx, ks_ref[0, 0]) * (X_HEAD ** -0.5)
    ps = []
    for hd in range(X_HEADS):
        sh = sc[:, hd * MEM_LEN:(hd + 1) * MEM_LEN]
        e = jnp.exp(sh - jnp.max(sh, axis=-1, keepdims=True))
        ps.append(e * (1.0 / jnp.sum(e, axis=-1, keepdims=True)))
    yx = _mm(jnp.concatenate(ps, axis=1), vs_ref[0, 0])

    merged = _sigmoid(zm_s[:, :D_MODEL]) * _mm(ya, wba_ref[0])
    merged = merged + _sigmoid(zm_s[:, D_MODEL:2 * D_MODEL]) * _mm(yb, wbb_ref[0])
    merged = merged + _sigmoid(zm_s[:, 2 * D_MODEL:]) * _mm(yx, wbx_ref[0])
    merged = merged + mc_ref[0]
    y = x + _mm(merged, wout_ref[0])
    if last:
        y = _rmsnorm(y, fg_ref[...])
    out_ref[0] = y


def _abx_call(layer, last, tm, x, mc, p, kst, vst, fg):
    bsz, seq, _ = x.shape
    row, full, win = _layer_specs(layer)
    tile = lambda wd: pl.BlockSpec((1, tm, wd), lambda b, j: (b, j, 0))
    mem = pl.BlockSpec((1, 1, X_HEADS * MEM_LEN, X_WIDTH), lambda b, j: (layer, b, 0, 0))
    in_specs = [tile(D_MODEL), tile(D_MODEL), row(D_MODEL),
                win(3 * A_WIDTH + 3 * B_WIDTH, p["o_a"]), win(X_WIDTH + 2 * D_MODEL, p["o_x"]),
                win(D_MODEL, p["o_mx"]),
                row(A_WIDTH), row(A_WIDTH), full(A_GROUPS, A_CHUNK, A_CHUNK), full(A_CHUNK, A_WIDTH),
                full(CONV_HIST, B_WIDTH), row(B_WIDTH), row(B_WIDTH), row(B_WIDTH),
                mem, mem,
                full(A_WIDTH, D_MODEL), full(B_WIDTH, D_MODEL), full(X_WIDTH, D_MODEL),
                full(D_MODEL, D_MODEL), pl.BlockSpec((1, D_MODEL), lambda b, j: (0, 0))]
    args = [x, mc, p["norm_g"], p["w_in"], p["w_in"], p["w_in"], p["a_ln_g"], p["a_ln_b"], p["a_ws"], p["a_bs"],
            p["b_wdw"], p["b_bdw"], p["b_ln_g"], p["b_ln_b"], kst, vst,
            p["wba"], p["wbb"], p["wbx"], p["wout"], fg]
    return pl.pallas_call(
        functools.partial(_abx_kernel, last, tm),
        grid=(bsz, seq // tm),
        in_specs=in_specs,
        out_specs=tile(D_MODEL),
        out_shape=jax.ShapeDtypeStruct((bsz, seq, D_MODEL), F32),
        scratch_shapes=[pltpu.VMEM((CONV_HIST + tm, B_WIDTH), F32), pltpu.VMEM((tm, 3 * D_MODEL), F32)],
        compiler_params=pltpu.CompilerParams(
            dimension_semantics=("arbitrary", "arbitrary"), vmem_limit_bytes=VMEM_LIMIT),
        name="abx_last" if last else "abx",
    )(*args)


def _pick_tile(seq, want, multiple):
    t = min(want, seq)
    while seq % t or t % multiple:
        t -= multiple
    return t


def kernel(x, mem, norm_g, w_in, w_in_vres, a_ln_g, a_ln_b, a_ws, a_bs, b_wdw, b_bdw, b_ln_g, b_ln_b, c_mu, c_mu_vres, c_w0, c_ww2, c_a0, c_wa2, c_v0, c_wv2, c_kk, c_ka, c_rk, c_gn_g, c_gn_b, mem_norm_g, w_mem_kv, w_branch_a, w_branch_b, w_branch_c, w_branch_x, w_out, final_norm_g):
    depth = w_in.shape[0]
    seq = x.shape[1]
    tc = _pick_tile(seq, 512, WKV_CHUNK)
    tm = _pick_tile(seq, 512, A_CHUNK)

    o_a = 0
    o_b = o_a + 3 * A_WIDTH
    o_c = o_b + 3 * B_WIDTH
    o_cg = o_c + C_SHIFT_WIDTH
    o_x = o_cg + C_WIDTH
    o_m = o_x + X_WIDTH

    rows = lambda a: a.reshape(depth, 1, -1)
    pad_to = lambda a, n, axis: jnp.pad(a, [(0, n - a.shape[i]) if i == axis else (0, 0) for i in range(a.ndim)])
    lead0 = lambda a: jnp.concatenate([jnp.zeros_like(a[:1]), a], axis=0)
    col_cat = lambda *a: jnp.concatenate(a, axis=2)

    w_in_bf = w_in.astype(BF16)
    zeros_lr = jnp.zeros_like(c_ww2)
    wkv_p = dict(
        norm_g=rows(norm_g), w_in=w_in_bf, o_c=o_c, o_cg=o_cg, o_mc=o_m + 2 * D_MODEL,
        w_vres=pad_to(lead0(w_in_vres), LANES, 2).astype(BF16),
        mu=rows(jnp.concatenate([c_mu, pad_to(lead0(c_mu_vres), LANES, 1)], axis=1)),
        w0=rows(c_w0), a0=rows(c_a0),
        wlr=jnp.concatenate([col_cat(c_ww2, zeros_lr), col_cat(zeros_lr, c_wa2)], axis=1).astype(BF16),
        kk=rows(c_kk), ka=rows(c_ka), rk=rows(c_rk), gn_g=rows(c_gn_g), gn_b=rows(c_gn_b),
        wbc=w_branch_c.astype(BF16),
        v0=rows(lead0(c_v0)), wv2=pad_to(lead0(c_wv2), LANES, 1).astype(BF16))
    abx_p = dict(
        norm_g=rows(norm_g), w_in=w_in_bf, o_a=o_a, o_x=o_x, o_mx=o_m + 3 * D_MODEL,
        a_ln_g=rows(a_ln_g), a_ln_b=rows(a_ln_b), a_ws=a_ws,
        a_bs=jnp.repeat(jnp.swapaxes(a_bs, 1, 2), A_WIDTH // A_GROUPS, axis=2),
        b_wdw=pad_to(b_wdw, CONV_HIST, 1), b_bdw=rows(b_bdw), b_ln_g=rows(b_ln_g), b_ln_b=rows(b_ln_b),
        wba=w_branch_a.astype(BF16), wbb=w_branch_b.astype(BF16), wbx=w_branch_x.astype(BF16),
        wout=w_out.astype(BF16))

    kst, vst = _memkv(mem, mem_norm_g, w_mem_kv)
    fg = final_norm_g.reshape(1, D_MODEL)
    v_first = None
    for i in range(depth):
        if i == 0:
            mc, v_first = _wkv_call(i, tc, x, wkv_p, None)
        else:
            (mc,) = _wkv_call(i, tc, x, wkv_p, v_first)
        x = _abx_call(i, i == depth - 1, tm, x, mc, abx_p, kst, vst, fg)
    return x
```
